```python
import jax, jax.numpy as jnp
from jax import lax
import numpy as np

D_MODEL = 1024
BATCH = 4
SEQ = 8192
DEPTH = 1
DEC_BATCH = 32
DEC_SEQ = 32
PAST_LEN = 1024

CHUNK = 64
HG_HEADS = 8
HG_DK = 128
HG_DV = 128
HG_WIDTH = HG_HEADS * HG_DV
POOL_WINDOWS = (2, 4, 8, 16)
POOL_GROUPS = 4
POOL_GC = 128
POOL_WIDTH = POOL_GROUPS * POOL_GC
POOL_BUF = 15
D_FF = -(-8 * D_MODEL // (3 * 256)) * 256
PLE_DIM = 256
IN_COLS = 4 * HG_WIDTH + POOL_WIDTH + 2 * D_MODEL
ALPHA = (2 * DEPTH) ** 0.25
BETA = (8 * DEPTH) ** -0.25
LN_EPS = 1e-5
RMS_EPS = 1e-6

kernel_name = "hgrn2_pool_gated_streaming_encoder_step"


def _layer_norm(x, g, b):
    xf = x.astype(jnp.float32)
    mu = jnp.mean(xf, axis=-1, keepdims=True)
    var = jnp.mean(jnp.square(xf - mu), axis=-1, keepdims=True)
    y = (xf - mu) * lax.rsqrt(var + LN_EPS) * g.astype(jnp.float32) + b.astype(jnp.float32)
    return y.astype(x.dtype)


def _hgrn_chunk(S0, q, logf, k, v):
    c = q.shape[2]
    b = jnp.cumsum(logf, axis=2)
    causal = jnp.tril(jnp.ones((c, c), dtype=bool))[:, :, None]
    decay = jnp.exp(jnp.where(causal, b[:, :, :, None, :] - b[:, :, None, :, :], -jnp.inf))
    scores = jnp.einsum("bhtk,bhtsk,bhsk->bhts", q, decay, k)
    o = (jnp.einsum("bhts,bhsv->bhtv", scores, v)
         + jnp.einsum("bhtk,bhkv->bhtv", q * jnp.exp(b), S0))
    b_last = b[:, :, -1:, :]
    S = (jnp.exp(b_last[:, :, 0, :, None]) * S0
         + jnp.einsum("bhsk,bhsv->bhkv", k * jnp.exp(b_last - b), v))
    return S, o


def _hgrn_recurrence(q, logf, k, v, S0):
    bsz, t, h, _ = q.shape
    cs = min(t, CHUNK)
    nc = t // cs

    def to_chunks(a):
        return a.reshape(bsz, nc, cs, h, a.shape[-1]).transpose(1, 0, 3, 2, 4)

    def step(S, inp):
        return _hgrn_chunk(S, *inp)

    S, o = lax.scan(step, S0, (to_chunks(q), to_chunks(logf), to_chunks(k), to_chunks(v)))
    o = o.transpose(1, 0, 3, 2, 4).reshape(bsz, t, h, HG_DV)
    return S, o


def _multiscale_pool(v, buf, offset):
    bsz, t, c = v.shape
    ext = jnp.concatenate([buf, v], axis=1)
    csum = jnp.concatenate([jnp.zeros((bsz, 1, c), jnp.float32), jnp.cumsum(ext, axis=1)], axis=1)
    pos = offset + jnp.arange(t)
    start = POOL_BUF + 1
    groups = []
    for j, w in enumerate(POOL_WINDOWS):
        lo, hi = j * POOL_GC, (j + 1) * POOL_GC
        win = csum[:, start:start + t, lo:hi] - csum[:, start - w:start - w + t, lo:hi]
        cnt = jnp.minimum(pos + 1, w).astype(jnp.float32)
        groups.append(win / cnt[None, :, None])
    pooled = jnp.concatenate(groups, axis=-1) - v
    return pooled, ext[:, -POOL_BUF:]


def _trunk_layer(x, p, S0, buf, offset, lb, w_in, hg_g, w_a, w_pm, p_scale, w_b, w_o,
                 ln1_g, ln1_b, w_up, w_down, w_pp, w_pg, ln2_g, ln2_b):
    bsz, t, _ = x.shape
    proj = x @ w_in
    splits = [HG_WIDTH, 2 * HG_WIDTH, 3 * HG_WIDTH, 4 * HG_WIDTH,
              4 * HG_WIDTH + POOL_WIDTH, 4 * HG_WIDTH + POOL_WIDTH + D_MODEL]
    q, fr, iv, g, v, ga, gb = jnp.split(proj, splits, axis=-1)

    lbh = lb.reshape(HG_HEADS, HG_DK)
    f = lbh + (1.0 - lbh) * jax.nn.sigmoid(fr.astype(jnp.float32).reshape(bsz, t, HG_HEADS, HG_DK))
    logf = jnp.log(f)
    k = 1.0 - f
    qf = q.astype(jnp.float32).reshape(bsz, t, HG_HEADS, HG_DK)
    vf = iv.astype(jnp.float32).reshape(bsz, t, HG_HEADS, HG_DV)
    S, o = _hgrn_recurrence(qf, logf, k, vf, S0.astype(jnp.float32))
    o = o * lax.rsqrt(jnp.mean(jnp.square(o), axis=-1, keepdims=True) + RMS_EPS)
    o = o * hg_g.astype(jnp.float32).reshape(HG_HEADS, HG_DV)
    o = o.reshape(bsz, t, HG_WIDTH) * jax.nn.silu(g.astype(jnp.float32))
    ya = o.astype(x.dtype) @ w_a

    pooled, new_buf = _multiscale_pool(v.astype(jnp.float32), buf.astype(jnp.float32), offset)
    pooled = jnp.einsum("btgc,gcd->btgd", pooled.reshape(bsz, t, POOL_GROUPS, POOL_GC),
                        w_pm.astype(jnp.float32)).reshape(bsz, t, POOL_WIDTH)
    yb = (pooled * p_scale.astype(jnp.float32)).astype(x.dtype) @ w_b

    m = jax.nn.sigmoid(ga) * ya + jax.nn.sigmoid(gb) * yb
    x = _layer_norm(ALPHA * x + m @ w_o, ln1_g, ln1_b)

    gt, up = jnp.split(x @ w_up, [D_FF], axis=-1)
    ffn = (jax.nn.silu(gt) * up) @ w_down
    ple = jax.nn.sigmoid(x @ w_pg) * (p @ w_pp)
    x = _layer_norm(ALPHA * x + ffn + ple, ln2_g, ln2_b)
    return x, S.astype(S0.dtype), new_buf.astype(buf.dtype)


def _normal(key, shape, scale):
    return scale * jax.random.normal(key, shape, jnp.float32)


def setup_inputs(seed: int = 0) -> dict:
    key = jax.random.key(seed)
    ks = jax.random.split(key, 24)
    return {
        "x_prompt": _normal(ks[0], (BATCH, SEQ, D_MODEL), 1.0),
        "x_sample": _normal(ks[1], (DEC_BATCH, DEC_SEQ, D_MODEL), 1.0),
        "p_prompt": _normal(ks[2], (DEPTH, BATCH, SEQ, PLE_DIM), 1.0),
        "p_sample": _normal(ks[3], (DEPTH, DEC_BATCH, DEC_SEQ, PLE_DIM), 1.0),
        "state_hgrn": _normal(ks[4], (DEPTH, DEC_BATCH, HG_HEADS, HG_DK, HG_DV), 0.5),
        "state_pool": _normal(ks[5], (DEPTH, DEC_BATCH, POOL_BUF, POOL_WIDTH), 1.0),
        "ln_in_g": 1.0 + _normal(ks[6], (D_MODEL,), 0.02),
        "ln_in_b": _normal(ks[7], (D_MODEL,), 0.02),
        "lb_logits": _normal(ks[8], (DEPTH + 1, HG_WIDTH), 0.1),
        "w_in": _normal(ks[9], (DEPTH, D_MODEL, IN_COLS), D_MODEL ** -0.5),
        "hgrn_norm_g": 1.0 + _normal(ks[10], (DEPTH, HG_WIDTH), 0.02),
        "w_branch_a": _normal(ks[11], (DEPTH, HG_WIDTH, D_MODEL), BETA * HG_WIDTH ** -0.5),
        "w_pool_mix": _normal(ks[12], (DEPTH, POOL_GROUPS, POOL_GC, POOL_GC), POOL_GC ** -0.5),
        "pool_scale": 1.0 + _normal(ks[13], (DEPTH, POOL_WIDTH), 0.02),
        "w_branch_b": _normal(ks[14], (DEPTH, POOL_WIDTH, D_MODEL), BETA * POOL_WIDTH ** -0.5),
        "w_out": _normal(ks[15], (DEPTH, D_MODEL, D_MODEL), BETA * D_MODEL ** -0.5),
        "ln1_g": 1.0 + _normal(ks[16], (DEPTH, D_MODEL), 0.02),
        "ln1_b": _normal(ks[17], (DEPTH, D_MODEL), 0.02),
        "w_ffn_up": _normal(ks[18], (DEPTH, D_MODEL, 2 * D_FF), D_MODEL ** -0.5),
        "w_ffn_down": _normal(ks[19], (DEPTH, D_FF, D_MODEL), BETA * D_FF ** -0.5),
        "w_ple_proj": _normal(ks[20], (DEPTH, PLE_DIM, D_MODEL), BETA * PLE_DIM ** -0.5),
        "w_ple_gate": _normal(ks[21], (DEPTH, D_MODEL, D_MODEL), D_MODEL ** -0.5),
        "ln2_g": 1.0 + _normal(ks[22], (DEPTH, D_MODEL), 0.02),
        "ln2_b": _normal(ks[23], (DEPTH, D_MODEL), 0.02),
    }


def reference(x_prompt, x_sample, p_prompt, p_sample, state_hgrn, state_pool, ln_in_g, ln_in_b,
              lb_logits, w_in, hgrn_norm_g, w_branch_a, w_pool_mix, pool_scale, w_branch_b, w_out,
              ln1_g, ln1_b, w_ffn_up, w_ffn_down, w_ple_proj, w_ple_gate, ln2_g, ln2_b):
    lb_all = jnp.cumsum(jax.nn.softmax(lb_logits.astype(jnp.float32), axis=0), axis=0)[:DEPTH]
    xp = _layer_norm(x_prompt, ln_in_g, ln_in_b)
    xs = _layer_norm(x_sample, ln_in_g, ln_in_b)
    bp = xp.shape[0]
    hp_list, pp_list, hs_list, ps_list = [], [], [], []
    for i in range(DEPTH):
        params = (lb_all[i], w_in[i], hgrn_norm_g[i], w_branch_a[i], w_pool_mix[i], pool_scale[i],
                  w_branch_b[i], w_out[i], ln1_g[i], ln1_b[i], w_ffn_up[i], w_ffn_down[i],
                  w_ple_proj[i], w_ple_gate[i], ln2_g[i], ln2_b[i])
        S0p = jnp.zeros((bp, HG_HEADS, HG_DK, HG_DV), state_hgrn.dtype)
        buf0p = jnp.zeros((bp, POOL_BUF, POOL_WIDTH), state_pool.dtype)
        xp, Sp, bufp = _trunk_layer(xp, p_prompt[i], S0p, buf0p, 0, *params)
        xs, Ss, bufs = _trunk_layer(xs, p_sample[i], state_hgrn[i], state_pool[i], PAST_LEN, *params)
        hp_list.append(Sp)
        pp_list.append(bufp)
        hs_list.append(Ss)
        ps_list.append(bufs)
    new_state_hgrn_prompt = jnp.stack(hp_list, axis=0)
    new_state_pool_prompt = jnp.stack(pp_list, axis=0)
    new_state_hgrn_sample = jnp.stack(hs_list, axis=0)
    new_state_pool_sample = jnp.stack(ps_list, axis=0)
    return (xp, xs, new_state_hgrn_prompt, new_state_pool_prompt, new_state_hgrn_sample, new_state_pool_sample)
```

```python
import functools

import jax
import jax.numpy as jnp
from jax import lax
from jax.experimental import pallas as pl
from jax.experimental.pallas import tpu as pltpu

F32 = jnp.float32
BF16 = jnp.bfloat16

D_MODEL = 1024
HG_HEADS = 8
HG_DK = 128
HG_DV = 128
HG_WIDTH = HG_HEADS * HG_DV
POOL_WINDOWS = (2, 4, 8, 16)
POOL_GC = 128
POOL_WIDTH = len(POOL_WINDOWS) * POOL_GC
POOL_BUF = 15
POOL_CARRY = 16
LN_EPS = 1e-5
RMS_EPS = 1e-6
EXP_CLAMP = 80.0
VMEM_LIMIT_BYTES = 56 * 1024 * 1024

_Q0, _F0, _I0, _G0 = 0, HG_WIDTH, 2 * HG_WIDTH, 3 * HG_WIDTH
_V0 = 4 * HG_WIDTH
_GA0 = _V0 + POOL_WIDTH
_GB0 = _GA0 + D_MODEL
IN_COLS = _GB0 + D_MODEL


def _sigmoid(x):
    return 0.5 * jnp.tanh(0.5 * x) + 0.5


def _layer_norm(x, g, b):
    mu = jnp.mean(x, axis=-1, keepdims=True)
    xc = x - mu
    var = jnp.mean(xc * xc, axis=-1, keepdims=True)
    return xc * lax.rsqrt(var + LN_EPS) * g + b


def _dot(a, b):
    return jnp.dot(a, b, preferred_element_type=F32)


def _mixer_kernel(x_ref, s0_ref, pool0_ref, lng_ref, lnb_ref, lbl_ref, win_ref, hgg_ref, wa_ref,
                  wpm_ref, psc_ref, wb_ref, wo_ref, l1g_ref, l1b_ref,
                  y_ref, sout_ref, poolout_ref,
                  b_s, qd_s, kdm_s, kde_s, iv_s, o_s, ext_s, st_s, pm_s,
                  *, G, TT, C, offset, layer, apply_ln_in, alpha):
    tb = pl.program_id(1)
    ntb = pl.num_programs(1)
    R = G * TT
    n_chunks = TT // C

    x = x_ref[...].reshape(R, D_MODEL)
    xn = _layer_norm(x, lng_ref[...], lnb_ref[...]) if apply_ln_in else x
    xb = xn.astype(BF16)

    def proj(lo, hi):
        return _dot(xb, win_ref[:, lo:hi])

    lbl = lbl_ref[...]
    lbe = jnp.exp(lbl - jnp.max(lbl, axis=0, keepdims=True))
    lb = jnp.sum(lbe[0:layer + 1], axis=0, keepdims=True) / jnp.sum(lbe, axis=0, keepdims=True)

    q = proj(_Q0, _F0)
    f = lb + (1.0 - lb) * _sigmoid(proj(_F0, _I0))
    logf = jnp.log(f)
    k = 1.0 - f

    logf_hi = logf.astype(BF16)
    logf_lo = (logf - logf_hi.astype(F32)).astype(BF16)
    rr = lax.broadcasted_iota(jnp.int32, (R, R), 0)
    cc = lax.broadcasted_iota(jnp.int32, (R, R), 1)
    shift = C.bit_length() - 1
    tri_blk = jnp.where(((rr >> shift) == (cc >> shift)) & (cc <= rr), 1.0, 0.0).astype(BF16)
    b = _dot(tri_blk, logf_hi) + _dot(tri_blk, logf_lo)
    b_s[...] = b

    def chunk_rows(row_in_chunk):
        return jnp.concatenate(
            [jnp.broadcast_to(b_s[pl.ds(ch * C + row_in_chunk, 1), :], (C, HG_WIDTH)) for ch in range(R // C)],
            axis=0)

    bm = chunk_rows(C // 2 - 1)
    bc = chunk_rows(C - 1)
    qd_s[...] = (q * jnp.exp(jnp.minimum(b - bm, EXP_CLAMP))).astype(BF16)
    kdm_s[...] = (k * jnp.exp(jnp.minimum(bm - b, EXP_CLAMP))).astype(BF16)
    kde_s[...] = (k * jnp.exp(bc - b)).astype(BF16)
    iv_s[...] = proj(_I0, _G0).astype(BF16)

    ti = lax.broadcasted_iota(jnp.int32, (C, C), 0)
    si = lax.broadcasted_iota(jnp.int32, (C, C), 1)
    causal = si <= ti
    nt_dims = (((1,), (1,)), ((), ()))
    tn_dims = (((0,), (0,)), ((), ()))

    for g in range(G):
        @pl.when(tb == 0)
        def _():
            for h in range(HG_HEADS):
                st_s[h] = s0_ref[g, h].T

        for ch in range(n_chunks):
            r0 = g * TT + ch * C
            rows = pl.ds(r0, C)
            for h in range(HG_HEADS):
                hs = pl.ds(h * HG_DK, HG_DK)
                st = st_s[h]
                em = jnp.exp(b_s[pl.ds(r0 + C // 2 - 1, 1), hs])
                ec = jnp.exp(b_s[pl.ds(r0 + C - 1, 1), hs])
                v_h = iv_s[rows, hs]
                w = jnp.concatenate([(st * em).astype(BF16), kdm_s[rows, hs]], axis=0)
                res = lax.dot_general(qd_s[rows, hs], w, nt_dims, preferred_element_type=F32)
                p = jnp.where(causal, res[:, HG_DV:], 0.0).astype(BF16)
                o = _dot(p, v_h) + res[:, :HG_DV]
                st_s[h] = st * ec + lax.dot_general(v_h, kde_s[rows, hs], tn_dims, preferred_element_type=F32)
                ms = jnp.mean(o * o, axis=-1, keepdims=True)
                o_s[rows, hs] = o * lax.rsqrt(ms + RMS_EPS) * hgg_ref[:, hs]

        @pl.when(tb == ntb - 1)
        def _():
            for h in range(HG_HEADS):
                sout_ref[g, h] = st_s[h].T

    gate = proj(_G0, _V0)
    ya = _dot((o_s[...] * (gate * _sigmoid(gate))).astype(BF16), wa_ref[...])

    v = proj(_V0, _GA0)
    pos1 = (offset + 1 + tb * TT + lax.broadcasted_iota(jnp.int32, (TT, 1), 0)).astype(F32)
    for g in range(G):
        @pl.when(tb == 0)
        def _():
            ext_s[0:POOL_CARRY, :] = pool0_ref[g]

        ext_s[POOL_CARRY:POOL_CARRY + TT, :] = v[g * TT:(g + 1) * TT]
        for j, w_len in enumerate(POOL_WINDOWS):
            js = pl.ds(j * POOL_GC, POOL_GC)
            cur = ext_s[pl.ds(POOL_CARRY, TT), js]
            win = cur
            for d in range(1, w_len):
                win = win + ext_s[pl.ds(POOL_CARRY - d, TT), js]
            pooled = win / jnp.minimum(pos1, float(w_len)) - cur
            mixed = _dot(pooled.astype(BF16), wpm_ref[j]) * psc_ref[:, js]
            pm_s[pl.ds(g * TT, TT), js] = mixed.astype(BF16)
        tail = ext_s[TT:TT + POOL_CARRY, :]
        poolout_ref[g] = tail
        ext_s[0:POOL_CARRY, :] = tail
    yb = _dot(pm_s[...], wb_ref[...])

    m = _sigmoid(proj(_GA0, _GB0)) * ya + _sigmoid(proj(_GB0, IN_COLS)) * yb
    z = alpha * xn + _dot(m.astype(BF16), wo_ref[...])
    y_ref[...] = _layer_norm(z, l1g_ref[...], l1b_ref[...]).reshape(G, TT, D_MODEL)


def _ffn_kernel(x_ref, p_ref, wup_ref, wdn_ref, wpg_ref, wpp_ref, g_ref, b_ref, y_ref, *, d_ff, alpha):
    x = x_ref[...]
    xb = x.astype(BF16)
    gt = _dot(xb, wup_ref[:, :d_ff])
    up = _dot(xb, wup_ref[:, d_ff:])
    hid = (gt * _sigmoid(gt) * up).astype(BF16)
    ffn = _dot(hid, wdn_ref[...])
    ple = _sigmoid(_dot(xb, wpg_ref[...])) * _dot(p_ref[...].astype(BF16), wpp_ref[...])
    y_ref[...] = _layer_norm(alpha * x + ffn + ple, g_ref[...], b_ref[...])


def _const_spec(shape):
    zeros = (0,) * len(shape)
    return pl.BlockSpec(shape, lambda *_: zeros, pipeline_mode=pl.Buffered(1))


def _mixer(x, s0, pool0, consts, *, G, TT, C, offset, layer, apply_ln_in, alpha):
    nseq, T, _ = x.shape
    assert nseq % G == 0 and T % TT == 0 and TT % C == 0 and C % 16 == 0
    assert G == 1 or T == TT, "several sequences per block only when one block covers the sequence"
    R = G * TT
    grid = (nseq // G, T // TT)
    kern = functools.partial(_mixer_kernel, G=G, TT=TT, C=C, offset=offset, layer=layer,
                             apply_ln_in=apply_ln_in, alpha=alpha)
    in_specs = [
        pl.BlockSpec((G, TT, D_MODEL), lambda i, t: (i, t, 0)),
        pl.BlockSpec((G, HG_HEADS, HG_DK, HG_DV), lambda i, t: (i, 0, 0, 0)),
        pl.BlockSpec((G, POOL_CARRY, POOL_WIDTH), lambda i, t: (i, 0, 0)),
    ] + [_const_spec(c.shape) for c in consts]
    out_specs = [
        pl.BlockSpec((G, TT, D_MODEL), lambda i, t: (i, t, 0)),
        pl.BlockSpec((G, HG_HEADS, HG_DK, HG_DV), lambda i, t: (i, 0, 0, 0)),
        pl.BlockSpec((G, POOL_CARRY, POOL_WIDTH), lambda i, t: (i, 0, 0)),
    ]
    out_shape = [
        jax.ShapeDtypeStruct((nseq, T, D_MODEL), F32),
        jax.ShapeDtypeStruct((nseq, HG_HEADS, HG_DK, HG_DV), F32),
        jax.ShapeDtypeStruct((nseq, POOL_CARRY, POOL_WIDTH), F32),
    ]
    scratch = [
        pltpu.VMEM((R, HG_WIDTH), F32),
        pltpu.VMEM((R, HG_WIDTH), BF16),
        pltpu.VMEM((R, HG_WIDTH), BF16),
        pltpu.VMEM((R, HG_WIDTH), BF16),
        pltpu.VMEM((R, HG_WIDTH), BF16),
        pltpu.VMEM((R, HG_WIDTH), F32),
        pltpu.VMEM((POOL_CARRY + TT, POOL_WIDTH), F32),
        pltpu.VMEM((HG_HEADS, HG_DV, HG_DK), F32),
        pltpu.VMEM((R, POOL_WIDTH), BF16),
    ]
    return pl.pallas_call(
        kern,
        grid=grid,
        in_specs=in_specs,
        out_specs=out_specs,
        out_shape=out_shape,
        scratch_shapes=scratch,
        compiler_params=pltpu.CompilerParams(
            dimension_semantics=("parallel", "arbitrary"), vmem_limit_bytes=VMEM_LIMIT_BYTES),
        name="mixer",
    )(x, s0, pool0, *consts)


def _ffn(x, p, consts, *, TM, alpha):
    n, _ = x.shape
    assert n % TM == 0
    d_ff = consts[1].shape[0]
    kern = functools.partial(_ffn_kernel, d_ff=d_ff, alpha=alpha)
    return pl.pallas_call(
        kern,
        grid=(n // TM,),
        in_specs=[pl.BlockSpec((TM, D_MODEL), lambda i: (i, 0)),
                  pl.BlockSpec((TM, p.shape[1]), lambda i: (i, 0))] + [_const_spec(c.shape) for c in consts],
        out_specs=pl.BlockSpec((TM, D_MODEL), lambda i: (i, 0)),
        out_shape=jax.ShapeDtypeStruct((n, D_MODEL), F32),
        compiler_params=pltpu.CompilerParams(
            dimension_semantics=("parallel",), vmem_limit_bytes=VMEM_LIMIT_BYTES),
        name="ffn",
    )(x, p, *consts)


def kernel(x_prompt, x_sample, p_prompt, p_sample, state_hgrn, state_pool, ln_in_g, ln_in_b, lb_logits, w_in,
           hgrn_norm_g, w_branch_a, w_pool_mix, pool_scale, w_branch_b, w_out, ln1_g, ln1_b, w_ffn_up,
           w_ffn_down, w_ple_proj, w_ple_gate, ln2_g, ln2_b):
    depth = w_in.shape[0]
    alpha = float((2 * depth) ** 0.25)
    past_len = 1024
    bp, tp, _ = x_prompt.shape
    bs, ts, _ = x_sample.shape
    row = lambda a: a.reshape(1, -1).astype(F32)

    xp, xs = x_prompt, x_sample
    hp, pp, hs_, ps = [], [], [], []
    for i in range(depth):
        mixer_consts = (row(ln_in_g), row(ln_in_b), lb_logits.astype(F32), w_in[i].astype(BF16),
                        row(hgrn_norm_g[i]), w_branch_a[i].astype(BF16), w_pool_mix[i].astype(BF16),
                        row(pool_scale[i]), w_branch_b[i].astype(BF16), w_out[i].astype(BF16),
                        row(ln1_g[i]), row(ln1_b[i]))
        ffn_consts = (w_ffn_up[i].astype(BF16), w_ffn_down[i].astype(BF16), w_ple_gate[i].astype(BF16),
                      w_ple_proj[i].astype(BF16), row(ln2_g[i]), row(ln2_b[i]))
        s0p = jnp.zeros((bp, HG_HEADS, HG_DK, HG_DV), F32)
        pool0p = jnp.zeros((bp, POOL_CARRY, POOL_WIDTH), F32)
        pool0s = jnp.pad(state_pool[i].astype(F32), ((0, 0), (POOL_CARRY - POOL_BUF, 0), (0, 0)))

        xp, sp, poolp = _mixer(xp, s0p, pool0p, mixer_consts, G=1, TT=256, C=64, offset=0, layer=i,
                               apply_ln_in=(i == 0), alpha=alpha)
        xs, ss, pools = _mixer(xs, state_hgrn[i].astype(F32), pool0s, mixer_consts, G=4, TT=ts, C=min(ts, 64),
                               offset=past_len, layer=i, apply_ln_in=(i == 0), alpha=alpha)
        xp = _ffn(xp.reshape(bp * tp, D_MODEL), p_prompt[i].reshape(bp * tp, -1), ffn_consts, TM=256,
                  alpha=alpha).reshape(bp, tp, D_MODEL)
        xs = _ffn(xs.reshape(bs * ts, D_MODEL), p_sample[i].reshape(bs * ts, -1), ffn_consts, TM=256,
                  alpha=alpha).reshape(bs, ts, D_MODEL)
        hp.append(sp)
        pp.append(poolp[:, POOL_CARRY - POOL_BUF:])
        hs_.append(ss)
        ps.append(pools[:, POOL_CARRY - POOL_BUF:])
    return (xp, xs, jnp.stack(hp, axis=0), jnp.stack(pp, axis=0), jnp.stack(hs_, axis=0), jnp.stack(ps, axis=0))
```

```python
import functools

import jax
import jax.numpy as jnp
from jax import lax
from jax.experimental import pallas as pl
from jax.experimental.pallas import tpu as pltpu

F32 = jnp.float32
BF16 = jnp.bfloat16

D_MODEL = 1024
HG_HEADS = 8
HG_DK = 128
HG_DV = 128
HG_WIDTH = HG_HEADS * HG_DV
POOL_WINDOWS = (2, 4, 8, 16)
POOL_GC = 128
POOL_WIDTH = len(POOL_WINDOWS) * POOL_GC
POOL_BUF = 15
POOL_CARRY = 16
LN_EPS = 1e-5
RMS_EPS = 1e-6
EXP_CLAMP = 80.0
VMEM_LIMIT_BYTES = 56 * 1024 * 1024
CHUNK = 64
PAST_LEN = 1024
MIXER_ROWS = 256
FFN_ROWS = 512


def _seqs_per_block(nseq, chunk):
    g = max(1, min(nseq, MIXER_ROWS // chunk))
    while nseq % g:
        g -= 1
    return g

_Q0, _F0, _I0, _G0 = 0, HG_WIDTH, 2 * HG_WIDTH, 3 * HG_WIDTH
_V0 = 4 * HG_WIDTH
_GA0 = _V0 + POOL_WIDTH
_GB0 = _GA0 + D_MODEL
IN_COLS = _GB0 + D_MODEL


def _sigmoid(x):
    return 0.5 * jnp.tanh(0.5 * x) + 0.5


def _layer_norm(x, g, b):
    mu = jnp.mean(x, axis=-1, keepdims=True)
    xc = x - mu
    var = jnp.mean(xc * xc, axis=-1, keepdims=True)
    return xc * lax.rsqrt(var + LN_EPS) * g + b


def _dot(a, b):
    return jnp.dot(a, b, preferred_element_type=F32)


def _mixer_kernel(x_ref, s0_ref, pool0_ref, lng_ref, lnb_ref, lbl_ref, win_ref, hgg_ref, wa_ref,
                  wpm_ref, psc_ref, wb_ref, wo_ref, l1g_ref, l1b_ref,
                  y_ref, sout_ref, poolout_ref,
                  b_s, qd_s, kdm_s, kde_s, iv_s, o_s, ext_s, st_s, pm_s,
                  *, G, TT, C, offset, layer, apply_ln_in, alpha):
    tb = pl.program_id(1)
    ntb = pl.num_programs(1)
    R = G * TT
    n_chunks = TT // C

    x = x_ref[...].reshape(R, D_MODEL)
    xn = _layer_norm(x, lng_ref[...], lnb_ref[...]) if apply_ln_in else x
    xb = xn.astype(BF16)

    def proj(lo, hi):
        return _dot(xb, win_ref[:, lo:hi])

    lbl = lbl_ref[...]
    lbe = jnp.exp(lbl - jnp.max(lbl, axis=0, keepdims=True))
    lb = jnp.sum(lbe[0:layer + 1], axis=0, keepdims=True) / jnp.sum(lbe, axis=0, keepdims=True)

    q = proj(_Q0, _F0)
    f = lb + (1.0 - lb) * _sigmoid(proj(_F0, _I0))
    logf = jnp.log(f)
    k = 1.0 - f

    logf_hi = logf.astype(BF16)
    logf_lo = (logf - logf_hi.astype(F32)).astype(BF16)
    rr = lax.broadcasted_iota(jnp.int32, (R, R), 0)
    cc = lax.broadcasted_iota(jnp.int32, (R, R), 1)
    shift = C.bit_length() - 1
    tri_blk = jnp.where(((rr >> shift) == (cc >> shift)) & (cc <= rr), 1.0, 0.0).astype(BF16)
    b = _dot(tri_blk, logf_hi) + _dot(tri_blk, logf_lo)
    b_s[...] = b

    def chunk_rows(row_in_chunk):
        return jnp.concatenate(
            [jnp.broadcast_to(b_s[pl.ds(ch * C + row_in_chunk, 1), :], (C, HG_WIDTH)) for ch in range(R // C)],
            axis=0)

    bm = chunk_rows(C // 2 - 1)
    bc = chunk_rows(C - 1)
    qd_s[...] = (q * jnp.exp(jnp.minimum(b - bm, EXP_CLAMP))).astype(BF16)
    kdm_s[...] = (k * jnp.exp(jnp.minimum(bm - b, EXP_CLAMP))).astype(BF16)
    kde_s[...] = (k * jnp.exp(bc - b)).astype(BF16)
    iv_s[...] = proj(_I0, _G0).astype(BF16)

    ti = lax.broadcasted_iota(jnp.int32, (C, C), 0)
    si = lax.broadcasted_iota(jnp.int32, (C, C), 1)
    causal = si <= ti
    nt_dims = (((1,), (1,)), ((), ()))
    tn_dims = (((0,), (0,)), ((), ()))

    @pl.when(tb == 0)
    def _():
        for g in range(G):
            for h in range(HG_HEADS):
                st_s[g, h] = s0_ref[g, h].T

    for ch in range(n_chunks):
        pairs = [(g, h) for g in range(G) for h in range(HG_HEADS)]
        rows_of = lambda g: pl.ds(g * TT + ch * C, C)
        hs_of = lambda h: pl.ds(h * HG_DK, HG_DK)
        res = {}
        for g, h in pairs:
            rows, hs = rows_of(g), hs_of(h)
            em = jnp.exp(b_s[pl.ds(g * TT + ch * C + C // 2 - 1, 1), hs])
            w = jnp.concatenate([(st_s[g, h] * em).astype(BF16), kdm_s[rows, hs]], axis=0)
            res[g, h] = lax.dot_general(qd_s[rows, hs], w, nt_dims, preferred_element_type=F32)
        for g, h in pairs:
            rows, hs = rows_of(g), hs_of(h)
            ec = jnp.exp(b_s[pl.ds(g * TT + ch * C + C - 1, 1), hs])
            st_s[g, h] = st_s[g, h] * ec + lax.dot_general(iv_s[rows, hs], kde_s[rows, hs], tn_dims,
                                                           preferred_element_type=F32)
        for g, h in pairs:
            rows, hs = rows_of(g), hs_of(h)
            p = jnp.where(causal, res[g, h][:, HG_DV:], 0.0).astype(BF16)
            o = _dot(p, iv_s[rows, hs]) + res[g, h][:, :HG_DV]
            ms = jnp.mean(o * o, axis=-1, keepdims=True)
            o_s[rows, hs] = o * lax.rsqrt(ms + RMS_EPS) * hgg_ref[:, hs]

    @pl.when(tb == ntb - 1)
    def _():
        for g in range(G):
            for h in range(HG_HEADS):
                sout_ref[g, h] = st_s[g, h].T

    gate = proj(_G0, _V0)
    ya = _dot((o_s[...] * (gate * _sigmoid(gate))).astype(BF16), wa_ref[...])

    v = proj(_V0, _GA0)
    pos1 = (offset + 1 + tb * TT + lax.broadcasted_iota(jnp.int32, (TT, 1), 0)).astype(F32)
    @pl.when(tb == 0)
    def _():
        for g in range(G):
            ext_s[g, 0:POOL_CARRY, :] = pool0_ref[g]

    for g in range(G):
        ext_s[g, POOL_CARRY:POOL_CARRY + TT, :] = v[g * TT:(g + 1) * TT]
    for j, w_len in enumerate(POOL_WINDOWS):
        js = pl.ds(j * POOL_GC, POOL_GC)
        pooled = []
        for g in range(G):
            cur = ext_s[g, pl.ds(POOL_CARRY, TT), js]
            win = cur
            for d in range(1, w_len):
                win = win + ext_s[g, pl.ds(POOL_CARRY - d, TT), js]
            pooled.append(win / jnp.minimum(pos1, float(w_len)) - cur)
        pooled = jnp.concatenate(pooled, axis=0) if G > 1 else pooled[0]
        pm_s[:, js] = (_dot(pooled.astype(BF16), wpm_ref[j]) * psc_ref[:, js]).astype(BF16)
    for g in range(G):
        tail = ext_s[g, TT:TT + POOL_CARRY, :]
        poolout_ref[g] = tail
        ext_s[g, 0:POOL_CARRY, :] = tail
    yb = _dot(pm_s[...], wb_ref[...])

    m = _sigmoid(proj(_GA0, _GB0)) * ya + _sigmoid(proj(_GB0, IN_COLS)) * yb
    z = alpha * xn + _dot(m.astype(BF16), wo_ref[...])
    y_ref[...] = _layer_norm(z, l1g_ref[...], l1b_ref[...]).reshape(G, TT, D_MODEL)


def _ffn_kernel(x_ref, p_ref, wup_ref, wdn_ref, wpg_ref, wpp_ref, g_ref, b_ref, y_ref, *, d_ff, alpha):
    x = x_ref[...]
    xb = x.astype(BF16)
    gt = _dot(xb, wup_ref[:, :d_ff])
    up = _dot(xb, wup_ref[:, d_ff:])
    hid = (gt * _sigmoid(gt) * up).astype(BF16)
    ffn = _dot(hid, wdn_ref[...])
    ple = _sigmoid(_dot(xb, wpg_ref[...])) * _dot(p_ref[...].astype(BF16), wpp_ref[...])
    y_ref[...] = _layer_norm(alpha * x + ffn + ple, g_ref[...], b_ref[...])


def _const_spec(shape):
    zeros = (0,) * len(shape)
    return pl.BlockSpec(shape, lambda *_: zeros, pipeline_mode=pl.Buffered(1))


def _mixer(x, s0, pool0, consts, *, G, TT, C, offset, layer, apply_ln_in, alpha):
    nseq, T, _ = x.shape
    assert nseq % G == 0 and T % TT == 0 and TT % C == 0 and C % 16 == 0
    R = G * TT
    grid = (nseq // G, T // TT)
    kern = functools.partial(_mixer_kernel, G=G, TT=TT, C=C, offset=offset, layer=layer,
                             apply_ln_in=apply_ln_in, alpha=alpha)
    in_specs = [
        pl.BlockSpec((G, TT, D_MODEL), lambda i, t: (i, t, 0)),
        pl.BlockSpec((G, HG_HEADS, HG_DK, HG_DV), lambda i, t: (i, 0, 0, 0)),
        pl.BlockSpec((G, POOL_CARRY, POOL_WIDTH), lambda i, t: (i, 0, 0)),
    ] + [_const_spec(c.shape) for c in consts]
    out_specs = [
        pl.BlockSpec((G, TT, D_MODEL), lambda i, t: (i, t, 0)),
        pl.BlockSpec((G, HG_HEADS, HG_DK, HG_DV), lambda i, t: (i, 0, 0, 0)),
        pl.BlockSpec((G, POOL_CARRY, POOL_WIDTH), lambda i, t: (i, 0, 0)),
    ]
    out_shape = [
        jax.ShapeDtypeStruct((nseq, T, D_MODEL), F32),
        jax.ShapeDtypeStruct((nseq, HG_HEADS, HG_DK, HG_DV), F32),
        jax.ShapeDtypeStruct((nseq, POOL_CARRY, POOL_WIDTH), F32),
    ]
    scratch = [
        pltpu.VMEM((R, HG_WIDTH), F32),
        pltpu.VMEM((R, HG_WIDTH), BF16),
        pltpu.VMEM((R, HG_WIDTH), BF16),
        pltpu.VMEM((R, HG_WIDTH), BF16),
        pltpu.VMEM((R, HG_WIDTH), BF16),
        pltpu.VMEM((R, HG_WIDTH), F32),
        pltpu.VMEM((G, POOL_CARRY + TT, POOL_WIDTH), F32),
        pltpu.VMEM((G, HG_HEADS, HG_DV, HG_DK), F32),
        pltpu.VMEM((R, POOL_WIDTH), BF16),
    ]
    return pl.pallas_call(
        kern,
        grid=grid,
        in_specs=in_specs,
        out_specs=out_specs,
        out_shape=out_shape,
        scratch_shapes=scratch,
        compiler_params=pltpu.CompilerParams(
            dimension_semantics=("parallel", "arbitrary"), vmem_limit_bytes=VMEM_LIMIT_BYTES),
        name="mixer",
    )(x, s0, pool0, *consts)


def _ffn(x, p, consts, *, TM, alpha):
    n, _ = x.shape
    assert n % TM == 0
    d_ff = consts[1].shape[0]
    kern = functools.partial(_ffn_kernel, d_ff=d_ff, alpha=alpha)
    return pl.pallas_call(
        kern,
        grid=(n // TM,),
        in_specs=[pl.BlockSpec((TM, D_MODEL), lambda i: (i, 0)),
                  pl.BlockSpec((TM, p.shape[1]), lambda i: (i, 0))] + [_const_spec(c.shape) for c in consts],
        out_specs=pl.BlockSpec((TM, D_MODEL), lambda i: (i, 0)),
        out_shape=jax.ShapeDtypeStruct((n, D_MODEL), F32),
        compiler_params=pltpu.CompilerParams(
            dimension_semantics=("parallel",), vmem_limit_bytes=VMEM_LIMIT_BYTES),
        name="ffn",
    )(x, p, *consts)


def kernel(x_prompt, x_sample, p_prompt, p_sample, state_hgrn, state_pool, ln_in_g, ln_in_b, lb_logits, w_in,
           hgrn_norm_g, w_branch_a, w_pool_mix, pool_scale, w_branch_b, w_out, ln1_g, ln1_b, w_ffn_up,
           w_ffn_down, w_ple_proj, w_ple_gate, ln2_g, ln2_b):
    depth = w_in.shape[0]
    alpha = float((2 * depth) ** 0.25)
    bp, tp, _ = x_prompt.shape
    bs, ts, _ = x_sample.shape
    cp, cs = min(tp, CHUNK), min(ts, CHUNK)
    gp, gs = _seqs_per_block(bp, cp), _seqs_per_block(bs, cs)
    row = lambda a: a.reshape(1, -1).astype(F32)

    xp, xs = x_prompt, x_sample
    hp, pp, hs_, ps = [], [], [], []
    for i in range(depth):
        mixer_consts = (row(ln_in_g), row(ln_in_b), lb_logits.astype(F32), w_in[i].astype(BF16),
                        row(hgrn_norm_g[i]), w_branch_a[i].astype(BF16), w_pool_mix[i].astype(BF16),
                        row(pool_scale[i]), w_branch_b[i].astype(BF16), w_out[i].astype(BF16),
                        row(ln1_g[i]), row(ln1_b[i]))
        ffn_consts = (w_ffn_up[i].astype(BF16), w_ffn_down[i].astype(BF16), w_ple_gate[i].astype(BF16),
                      w_ple_proj[i].astype(BF16), row(ln2_g[i]), row(ln2_b[i]))
        s0p = jnp.zeros((bp, HG_HEADS, HG_DK, HG_DV), F32)
        pool0p = jnp.zeros((bp, POOL_CARRY, POOL_WIDTH), F32)
        pool0s = jnp.pad(state_pool[i].astype(F32), ((0, 0), (POOL_CARRY - POOL_BUF, 0), (0, 0)))

        xp, sp, poolp = _mixer(xp, s0p, pool0p, mixer_consts, G=gp, TT=cp, C=cp, offset=0, layer=i,
                               apply_ln_in=(i == 0), alpha=alpha)
        xs, ss, pools = _mixer(xs, state_hgrn[i].astype(F32), pool0s, mixer_consts, G=gs, TT=cs, C=cs,
                               offset=PAST_LEN, layer=i, apply_ln_in=(i == 0), alpha=alpha)
        xp = _ffn(xp.reshape(bp * tp, D_MODEL), p_prompt[i].reshape(bp * tp, -1), ffn_consts,
                  TM=min(FFN_ROWS, bp * tp), alpha=alpha).reshape(bp, tp, D_MODEL)
        xs = _ffn(xs.reshape(bs * ts, D_MODEL), p_sample[i].reshape(bs * ts, -1), ffn_consts,
                  TM=min(FFN_ROWS, bs * ts), alpha=alpha).reshape(bs, ts, D_MODEL)
        hp.append(sp)
        pp.append(poolp[:, POOL_CARRY - POOL_BUF:])
        hs_.append(ss)
        ps.append(pools[:, POOL_CARRY - POOL_BUF:])
    return (xp, xs, jnp.stack(hp, axis=0), jnp.stack(pp, axis=0), jnp.stack(hs_, axis=0), jnp.stack(ps, axis=0))
```

```python
import functools

import jax
import jax.numpy as jnp
from jax import lax
from jax.experimental import pallas as pl
from jax.experimental.pallas import tpu as pltpu

F32 = jnp.float32
BF16 = jnp.bfloat16

D_MODEL = 1024
HG_HEADS = 8
HG_DK = 128
HG_DV = 128
HG_WIDTH = HG_HEADS * HG_DV
POOL_WINDOWS = (2, 4, 8, 16)
POOL_GC = 128
POOL_WIDTH = len(POOL_WINDOWS) * POOL_GC
POOL_BUF = 15
POOL_CARRY = 16
LN_EPS = 1e-5
RMS_EPS = 1e-6
EXP_CLAMP = 80.0
VMEM_LIMIT_BYTES = 56 * 1024 * 1024
CHUNK = 64
PAST_LEN = 1024
MIXER_ROWS = 256
FFN_ROWS = 512


def _seqs_per_block(nseq, chunk):
    g = max(1, min(nseq, MIXER_ROWS // chunk))
    while nseq % g:
        g -= 1
    return g


_Q0, _F0, _I0, _G0 = 0, HG_WIDTH, 2 * HG_WIDTH, 3 * HG_WIDTH
_V0 = 4 * HG_WIDTH
_GA0 = _V0 + POOL_WIDTH
_GB0 = _GA0 + D_MODEL
IN_COLS = _GB0 + D_MODEL


def _sigmoid(x):
    return 0.5 * jnp.tanh(0.5 * x) + 0.5


def _layer_norm(x, g, b):
    mu = jnp.mean(x, axis=-1, keepdims=True)
    xc = x - mu
    var = jnp.mean(xc * xc, axis=-1, keepdims=True)
    return xc * lax.rsqrt(var + LN_EPS) * g + b


def _dot(a, b):
    return jnp.dot(a, b, preferred_element_type=F32)


def _mixer_kernel(x_ref, s0_ref, pool0_ref, lng_ref, lnb_ref, lbl_ref, win_ref, hgg_ref, wa_ref,
                  wpm_ref, psc_ref, wb_ref, wo_ref, l1g_ref, l1b_ref,
                  y_ref, sout_ref, poolout_ref,
                  b_s, qd_s, kdm_s, kde_s, iv_s, o_s, hist_s, st_s, pm_s,
                  *, G, TT, C, offset, layer, apply_ln_in, alpha):
    tb = pl.program_id(1)
    ntb = pl.num_programs(1)
    R = G * TT
    n_chunks = TT // C
    pairs = [(g, h) for g in range(G) for h in range(HG_HEADS)]
    hs_of = lambda h: pl.ds(h * HG_DK, HG_DK)
    nt_dims = (((1,), (1,)), ((), ()))
    tn_dims = (((0,), (0,)), ((), ()))

    @pl.when(tb == 0)
    def _():
        for g, h in pairs:
            st_s[g, h] = s0_ref[g, h].T
        for g in range(G):
            hist_s[g] = pool0_ref[g]

    x = x_ref[...].reshape(R, D_MODEL)
    xn = _layer_norm(x, lng_ref[...], lnb_ref[...]) if apply_ln_in else x
    xb = xn.astype(BF16)

    def proj(lo, hi):
        return _dot(xb, win_ref[:, lo:hi])

    lbl = lbl_ref[...]
    lbe = jnp.exp(lbl - jnp.max(lbl, axis=0, keepdims=True))
    lb = jnp.sum(lbe[0:layer + 1], axis=0, keepdims=True) / jnp.sum(lbe, axis=0, keepdims=True)

    q = proj(_Q0, _F0)
    f = lb + (1.0 - lb) * _sigmoid(proj(_F0, _I0))
    logf = jnp.log(f)
    k = 1.0 - f
    iv_s[...] = proj(_I0, _G0).astype(BF16)

    logf_hi = logf.astype(BF16)
    logf_lo = (logf - logf_hi.astype(F32)).astype(BF16)
    rr = lax.broadcasted_iota(jnp.int32, (R, R), 0)
    cc = lax.broadcasted_iota(jnp.int32, (R, R), 1)
    shift = C.bit_length() - 1
    tri_blk = jnp.where(((rr >> shift) == (cc >> shift)) & (cc <= rr), 1.0, 0.0).astype(BF16)
    b = _dot(tri_blk, logf_hi) + _dot(tri_blk, logf_lo)
    b_s[...] = b

    def chunk_rows(row_in_chunk):
        return jnp.concatenate(
            [jnp.broadcast_to(b_s[pl.ds(ch * C + row_in_chunk, 1), :], (C, HG_WIDTH)) for ch in range(R // C)],
            axis=0)

    bm = chunk_rows(C // 2 - 1)
    bc = chunk_rows(C - 1)
    qd_s[...] = (q * jnp.exp(jnp.minimum(b - bm, EXP_CLAMP))).astype(BF16)
    kdm_s[...] = (k * jnp.exp(jnp.minimum(bm - b, EXP_CLAMP))).astype(BF16)
    kde_s[...] = (k * jnp.exp(bc - b)).astype(BF16)

    ti = lax.broadcasted_iota(jnp.int32, (C, C), 0)
    si = lax.broadcasted_iota(jnp.int32, (C, C), 1)
    causal = si <= ti

    def stage_scores(ch):
        res = {}
        for g, h in pairs:
            rows, hs = pl.ds(g * TT + ch * C, C), hs_of(h)
            em = jnp.exp(b_s[pl.ds(g * TT + ch * C + C // 2 - 1, 1), hs])
            w = jnp.concatenate([(st_s[g, h] * em).astype(BF16), kdm_s[rows, hs]], axis=0)
            res[g, h] = lax.dot_general(qd_s[rows, hs], w, nt_dims, preferred_element_type=F32)
        return res

    def stage_state(ch):
        for g, h in pairs:
            rows, hs = pl.ds(g * TT + ch * C, C), hs_of(h)
            ec = jnp.exp(b_s[pl.ds(g * TT + ch * C + C - 1, 1), hs])
            st_s[g, h] = st_s[g, h] * ec + lax.dot_general(iv_s[rows, hs], kde_s[rows, hs], tn_dims,
                                                           preferred_element_type=F32)

    def stage_out(ch, res):
        for g, h in pairs:
            rows, hs = pl.ds(g * TT + ch * C, C), hs_of(h)
            p = jnp.where(causal, res[g, h][:, HG_DV:], 0.0).astype(BF16)
            o = _dot(p, iv_s[rows, hs]) + res[g, h][:, :HG_DV]
            ms = jnp.mean(o * o, axis=-1, keepdims=True)
            o_s[rows, hs] = o * lax.rsqrt(ms + RMS_EPS) * hgg_ref[:, hs]

    res0 = stage_scores(0)

    v = proj(_V0, _GA0)
    pos1 = (offset + 1 + tb * TT + lax.broadcasted_iota(jnp.int32, (TT, 1), 0)).astype(F32)
    wins = [[] for _ in POOL_WINDOWS]
    for g in range(G):
        ext = jnp.concatenate([hist_s[g], v[g * TT:(g + 1) * TT]], axis=0)
        tail = ext[TT:TT + POOL_CARRY]
        poolout_ref[g] = tail
        hist_s[g] = tail
        acc = ext
        for j, w_len in enumerate(POOL_WINDOWS):
            acc = acc + pltpu.roll(acc, w_len // 2, 0)
            wins[j].append(acc[POOL_CARRY:, :POOL_GC])
            acc = acc[:, POOL_GC:]
    for j, w_len in enumerate(POOL_WINDOWS):
        js = pl.ds(j * POOL_GC, POOL_GC)
        win = jnp.concatenate(wins[j], axis=0) if G > 1 else wins[j][0]
        cnt = jnp.minimum(pos1, float(w_len))
        cnt = jnp.concatenate([cnt] * G, axis=0) if G > 1 else cnt
        pooled = win / cnt - v[:, j * POOL_GC:(j + 1) * POOL_GC]
        pm_s[:, js] = (_dot(pooled.astype(BF16), wpm_ref[j]) * psc_ref[:, js]).astype(BF16)
    yb = _sigmoid(proj(_GB0, IN_COLS)) * _dot(pm_s[...], wb_ref[...])

    stage_state(0)
    gate = proj(_G0, _V0)
    gate = gate * _sigmoid(gate)
    ga = _sigmoid(proj(_GA0, _GB0))
    stage_out(0, res0)
    for ch in range(1, n_chunks):
        res = stage_scores(ch)
        stage_state(ch)
        stage_out(ch, res)

    ya = _dot((o_s[...] * gate).astype(BF16), wa_ref[...])
    m = ga * ya + yb
    z = alpha * xn + _dot(m.astype(BF16), wo_ref[...])
    y_ref[...] = _layer_norm(z, l1g_ref[...], l1b_ref[...]).reshape(G, TT, D_MODEL)

    @pl.when(tb == ntb - 1)
    def _():
        for g, h in pairs:
            sout_ref[g, h] = st_s[g, h].T


def _ffn_kernel(x_ref, p_ref, wup_ref, wdn_ref, wpg_ref, wpp_ref, g_ref, b_ref, y_ref, *, d_ff, alpha):
    x = x_ref[...]
    xb = x.astype(BF16)
    gt = _dot(xb, wup_ref[:, :d_ff])
    up = _dot(xb, wup_ref[:, d_ff:])
    hid = (gt * _sigmoid(gt) * up).astype(BF16)
    ffn = _dot(hid, wdn_ref[...])
    ple = _sigmoid(_dot(xb, wpg_ref[...])) * _dot(p_ref[...].astype(BF16), wpp_ref[...])
    y_ref[...] = _layer_norm(alpha * x + ffn + ple, g_ref[...], b_ref[...])


def _const_spec(shape):
    zeros = (0,) * len(shape)
    return pl.BlockSpec(shape, lambda *_: zeros, pipeline_mode=pl.Buffered(1))


def _mixer(x, s0, pool0, consts, *, G, TT, C, offset, layer, apply_ln_in, alpha):
    nseq, T, _ = x.shape
    assert nseq % G == 0 and T % TT == 0 and TT % C == 0 and C % 16 == 0
    R = G * TT
    grid = (nseq // G, T // TT)
    kern = functools.partial(_mixer_kernel, G=G, TT=TT, C=C, offset=offset, layer=layer,
                             apply_ln_in=apply_ln_in, alpha=alpha)
    in_specs = [
        pl.BlockSpec((G, TT, D_MODEL), lambda i, t: (i, t, 0)),
        pl.BlockSpec((G, HG_HEADS, HG_DK, HG_DV), lambda i, t: (i, 0, 0, 0)),
        pl.BlockSpec((G, POOL_CARRY, POOL_WIDTH), lambda i, t: (i, 0, 0)),
    ] + [_const_spec(c.shape) for c in consts]
    out_specs = [
        pl.BlockSpec((G, TT, D_MODEL), lambda i, t: (i, t, 0)),
        pl.BlockSpec((G, HG_HEADS, HG_DK, HG_DV), lambda i, t: (i, 0, 0, 0)),
        pl.BlockSpec((G, POOL_CARRY, POOL_WIDTH), lambda i, t: (i, 0, 0)),
    ]
    out_shape = [
        jax.ShapeDtypeStruct((nseq, T, D_MODEL), F32),
        jax.ShapeDtypeStruct((nseq, HG_HEADS, HG_DK, HG_DV), F32),
        jax.ShapeDtypeStruct((nseq, POOL_CARRY, POOL_WIDTH), F32),
    ]
    scratch = [
        pltpu.VMEM((R, HG_WIDTH), F32),
        pltpu.VMEM((R, HG_WIDTH), BF16),
        pltpu.VMEM((R, HG_WIDTH), BF16),
        pltpu.VMEM((R, HG_WIDTH), BF16),
        pltpu.VMEM((R, HG_WIDTH), BF16),
        pltpu.VMEM((R, HG_WIDTH), F32),
        pltpu.VMEM((G, POOL_CARRY, POOL_WIDTH), F32),
        pltpu.VMEM((G, HG_HEADS, HG_DV, HG_DK), F32),
        pltpu.VMEM((R, POOL_WIDTH), BF16),
    ]
    return pl.pallas_call(
        kern,
        grid=grid,
        in_specs=in_specs,
        out_specs=out_specs,
        out_shape=out_shape,
        scratch_shapes=scratch,
        compiler_params=pltpu.CompilerParams(
            dimension_semantics=("parallel", "arbitrary"), vmem_limit_bytes=VMEM_LIMIT_BYTES),
        name="mixer",
    )(x, s0, pool0, *consts)


def _ffn(x, p, consts, *, TM, alpha):
    n, _ = x.shape
    assert n % TM == 0
    d_ff = consts[1].shape[0]
    kern = functools.partial(_ffn_kernel, d_ff=d_ff, alpha=alpha)
    return pl.pallas_call(
        kern,
        grid=(n // TM,),
        in_specs=[pl.BlockSpec((TM, D_MODEL), lambda i: (i, 0)),
                  pl.BlockSpec((TM, p.shape[1]), lambda i: (i, 0))] + [_const_spec(c.shape) for c in consts],
        out_specs=pl.BlockSpec((TM, D_MODEL), lambda i: (i, 0)),
        out_shape=jax.ShapeDtypeStruct((n, D_MODEL), F32),
        compiler_params=pltpu.CompilerParams(
            dimension_semantics=("parallel",), vmem_limit_bytes=VMEM_LIMIT_BYTES),
        name="ffn",
    )(x, p, *consts)


def kernel(x_prompt, x_sample, p_prompt, p_sample, state_hgrn, state_pool, ln_in_g, ln_in_b, lb_logits, w_in,
           hgrn_norm_g, w_branch_a, w_pool_mix, pool_scale, w_branch_b, w_out, ln1_g, ln1_b, w_ffn_up,
           w_ffn_down, w_ple_proj, w_ple_gate, ln2_g, ln2_b):
    depth = w_in.shape[0]
    alpha = float((2 * depth) ** 0.25)
    bp, tp, _ = x_prompt.shape
    bs, ts, _ = x_sample.shape
    cp, cs = min(tp, CHUNK), min(ts, CHUNK)
    gp, gs = _seqs_per_block(bp, cp), _seqs_per_block(bs, cs)
    row = lambda a: a.reshape(1, -1).astype(F32)

    xp, xs = x_prompt, x_sample
    hp, pp, hs_, ps = [], [], [], []
    for i in range(depth):
        mixer_consts = (row(ln_in_g), row(ln_in_b), lb_logits.astype(F32), w_in[i].astype(BF16),
                        row(hgrn_norm_g[i]), w_branch_a[i].astype(BF16), w_pool_mix[i].astype(BF16),
                        row(pool_scale[i]), w_branch_b[i].astype(BF16), w_out[i].astype(BF16),
                        row(ln1_g[i]), row(ln1_b[i]))
        ffn_consts = (w_ffn_up[i].astype(BF16), w_ffn_down[i].astype(BF16), w_ple_gate[i].astype(BF16),
                      w_ple_proj[i].astype(BF16), row(ln2_g[i]), row(ln2_b[i]))
        s0p = jnp.zeros((bp, HG_HEADS, HG_DK, HG_DV), F32)
        pool0p = jnp.zeros((bp, POOL_CARRY, POOL_WIDTH), F32)
        pool0s = jnp.pad(state_pool[i].astype(F32), ((0, 0), (POOL_CARRY - POOL_BUF, 0), (0, 0)))

        xp, sp, poolp = _mixer(xp, s0p, pool0p, mixer_consts, G=gp, TT=cp, C=cp, offset=0, layer=i,
                               apply_ln_in=(i == 0), alpha=alpha)
        xs, ss, pools = _mixer(xs, state_hgrn[i].astype(F32), pool0s, mixer_consts, G=gs, TT=cs, C=cs,
                               offset=PAST_LEN, layer=i, apply_ln_in=(i == 0), alpha=alpha)
        xp = _ffn(xp.reshape(bp * tp, D_MODEL), p_prompt[i].reshape(bp * tp, -1), ffn_consts,
                  TM=min(FFN_ROWS, bp * tp), alpha=alpha).reshape(bp, tp, D_MODEL)
        xs = _ffn(xs.reshape(bs * ts, D_MODEL), p_sample[i].reshape(bs * ts, -1), ffn_consts,
                  TM=min(FFN_ROWS, bs * ts), alpha=alpha).reshape(bs, ts, D_MODEL)
        hp.append(sp)
        pp.append(poolp[:, POOL_CARRY - POOL_BUF:])
        hs_.append(ss)
        ps.append(pools[:, POOL_CARRY - POOL_BUF:])
    return (xp, xs, jnp.stack(hp, axis=0), jnp.stack(pp, axis=0), jnp.stack(hs_, axis=0), jnp.stack(ps, axis=0))
```

```python
import functools

import jax
import jax.numpy as jnp
from jax import lax
from jax.experimental import pallas as pl
from jax.experimental.pallas import tpu as pltpu

F32 = jnp.float32
BF16 = jnp.bfloat16

D_MODEL = 1024
HG_HEADS = 8
HG_DK = 128
HG_DV = 128
HG_WIDTH = HG_HEADS * HG_DV
POOL_WINDOWS = (2, 4, 8, 16)
POOL_GC = 128
POOL_WIDTH = len(POOL_WINDOWS) * POOL_GC
POOL_BUF = 15
SUBLANES = 8
POOL_CARRY = 16
LN_EPS = 1e-5
RMS_EPS = 1e-6
EXP_CLAMP = 80.0
VMEM_LIMIT_BYTES = 56 * 1024 * 1024
CHUNK = 64
PAST_LEN = 1024
MIXER_ROWS = 256
FFN_ROWS = 512


def _seqs_per_block(nseq, chunk):
    g = max(1, min(nseq, MIXER_ROWS // chunk))
    while nseq % g:
        g -= 1
    return g


_Q0, _F0, _I0, _G0 = 0, HG_WIDTH, 2 * HG_WIDTH, 3 * HG_WIDTH
_V0 = 4 * HG_WIDTH
_GA0 = _V0 + POOL_WIDTH
_GB0 = _GA0 + D_MODEL
IN_COLS = _GB0 + D_MODEL


def _sigmoid(x):
    return 0.5 * jnp.tanh(0.5 * x) + 0.5


def _layer_norm(x, g, b):
    mu = jnp.mean(x, axis=-1, keepdims=True)
    xc = x - mu
    var = jnp.mean(xc * xc, axis=-1, keepdims=True)
    return xc * lax.rsqrt(var + LN_EPS) * g + b


def _dot(a, b):
    return jnp.dot(a, b, preferred_element_type=F32)


def _mixer_kernel(x_ref, s0_ref, pool0_ref, lng_ref, lnb_ref, lbl_ref, win_ref, hgg_ref, wa_ref,
                  wpm_ref, psc_ref, wb_ref, wo_ref, l1g_ref, l1b_ref,
                  y_ref, sout_ref, poolout_ref,
                  xn_s, xb_s, qd_s, kdm_s, kde_s, iv_s, em_s, ec_s, o_s, hist_s, st_s, pm_s,
                  *, G, C, offset, layer, apply_ln_in, alpha, skew):
    step = pl.program_id(1)
    nsteps = pl.num_programs(1)
    R = G * C
    pairs = [(g, h) for g in range(G) for h in range(HG_HEADS)]
    hs_of = lambda h: pl.ds(h * HG_DK, HG_DK)
    rows_of = lambda g: pl.ds(g * C, C)
    nt_dims = (((1,), (1,)), ((), ()))
    tn_dims = (((0,), (0,)), ((), ()))
    first_b_step = 1 if skew else 0

    if skew:
        @pl.when(step == 0)
        def _():
            for ref in (xn_s, xb_s, qd_s, kdm_s, kde_s, iv_s, em_s, ec_s, hist_s, st_s):
                ref[...] = jnp.zeros(ref.shape, ref.dtype)

    @pl.when(step == first_b_step)
    def _():
        for g, h in pairs:
            st_s[g, h] = s0_ref[g, h].T
        for g in range(G):
            hist_s[g] = pool0_ref[g]

    def proj(xb, lo, hi):
        return _dot(xb, win_ref[:, lo:hi])

    a = {}

    def a_norm():
        x = x_ref[...].reshape(R, D_MODEL)
        a["xn"] = _layer_norm(x, lng_ref[...], lnb_ref[...]) if apply_ln_in else x
        a["xb"] = a["xn"].astype(BF16)

    def a_gates():
        lbl = lbl_ref[...]
        lbe = jnp.exp(lbl - jnp.max(lbl, axis=0, keepdims=True))
        lb = jnp.sum(lbe[0:layer + 1], axis=0, keepdims=True) / jnp.sum(lbe, axis=0, keepdims=True)
        a["q"] = proj(a["xb"], _Q0, _F0)
        f = lb + (1.0 - lb) * _sigmoid(proj(a["xb"], _F0, _I0))
        a["logf"] = jnp.log(f)
        a["k"] = 1.0 - f

    def a_values():
        a["iv"] = proj(a["xb"], _I0, _G0).astype(BF16)

    def a_cumsum():
        logf = a["logf"]
        logf_hi = logf.astype(BF16)
        logf_lo = (logf - logf_hi.astype(F32)).astype(BF16)
        rr = lax.broadcasted_iota(jnp.int32, (R, R), 0)
        cc = lax.broadcasted_iota(jnp.int32, (R, R), 1)
        shift = C.bit_length() - 1
        tri_blk = jnp.where(((rr >> shift) == (cc >> shift)) & (cc <= rr), 1.0, 0.0).astype(BF16)
        a["b"] = _dot(tri_blk, logf_hi) + _dot(tri_blk, logf_lo)

    def a_decay():
        b, q, k = a["b"], a["q"], a["k"]
        bm_rows = [b[g * C + C // 2 - 1:g * C + C // 2, :] for g in range(G)]
        bc_rows = [b[g * C + C - 1:g * C + C, :] for g in range(G)]
        bm = jnp.concatenate([jnp.broadcast_to(r, (C, HG_WIDTH)) for r in bm_rows], axis=0)
        bc = jnp.concatenate([jnp.broadcast_to(r, (C, HG_WIDTH)) for r in bc_rows], axis=0)
        a["qd"] = (q * jnp.exp(jnp.minimum(b - bm, EXP_CLAMP))).astype(BF16)
        a["kdm"] = (k * jnp.exp(jnp.minimum(bm - b, EXP_CLAMP))).astype(BF16)
        a["kde"] = (k * jnp.exp(bc - b)).astype(BF16)
        a["em"] = [jnp.exp(r) for r in bm_rows]
        a["ec"] = [jnp.exp(r) for r in bc_rows]

    def a_store():
        xn_s[...] = a["xn"]
        xb_s[...] = a["xb"]
        qd_s[...] = a["qd"]
        kdm_s[...] = a["kdm"]
        kde_s[...] = a["kde"]
        iv_s[...] = a["iv"]
        for g in range(G):
            em_s[pl.ds(g, 1), :] = a["em"][g]
            ec_s[pl.ds(g, 1), :] = a["ec"][g]

    bv = {}

    def b_scores():
        res = {}
        for g, h in pairs:
            rows, hs = rows_of(g), hs_of(h)
            w = jnp.concatenate([(st_s[g, h] * em_s[pl.ds(g, 1), hs]).astype(BF16), kdm_s[rows, hs]], axis=0)
            res[g, h] = lax.dot_general(qd_s[rows, hs], w, nt_dims, preferred_element_type=F32)
        bv["res"] = res

    def b_state():
        for g, h in pairs:
            rows, hs = rows_of(g), hs_of(h)
            st_s[g, h] = st_s[g, h] * ec_s[pl.ds(g, 1), hs] + lax.dot_general(
                iv_s[rows, hs], kde_s[rows, hs], tn_dims, preferred_element_type=F32)

    def b_out():
        ti = lax.broadcasted_iota(jnp.int32, (C, C), 0)
        si = lax.broadcasted_iota(jnp.int32, (C, C), 1)
        causal = si <= ti
        for g, h in pairs:
            rows, hs = rows_of(g), hs_of(h)
            res = bv["res"][g, h]
            p = jnp.where(causal, res[:, HG_DV:], 0.0).astype(BF16)
            o = _dot(p, iv_s[rows, hs]) + res[:, :HG_DV]
            ms = jnp.mean(o * o, axis=-1, keepdims=True)
            o_s[rows, hs] = o * lax.rsqrt(ms + RMS_EPS) * hgg_ref[:, hs]

    def b_pool():
        xb = xb_s[...]
        v = proj(xb, _V0, _GA0)
        tb = step - first_b_step
        pos1 = (offset + 1 + tb * C + lax.broadcasted_iota(jnp.int32, (C, 1), 0)).astype(F32)
        wins = [[] for _ in POOL_WINDOWS]
        for g in range(G):
            ext = jnp.concatenate([hist_s[g], v[g * C:(g + 1) * C]], axis=0)
            tail = ext[C:C + POOL_CARRY]
            poolout_ref[g] = tail
            hist_s[g] = tail
            acc = ext
            for j, w_len in enumerate(POOL_WINDOWS):
                acc = acc + pltpu.roll(acc, w_len // 2, 0)
                wins[j].append(acc[POOL_CARRY:, :POOL_GC])
                acc = acc[:, POOL_GC:]
        for j, w_len in enumerate(POOL_WINDOWS):
            js = pl.ds(j * POOL_GC, POOL_GC)
            win = jnp.concatenate(wins[j], axis=0) if G > 1 else wins[j][0]
            cnt = jnp.clip(pos1, 1.0, float(w_len))
            cnt = jnp.concatenate([cnt] * G, axis=0) if G > 1 else cnt
            pooled = win / cnt - v[:, j * POOL_GC:(j + 1) * POOL_GC]
            pm_s[:, js] = (_dot(pooled.astype(BF16), wpm_ref[j]) * psc_ref[:, js]).astype(BF16)
        bv["yb"] = _sigmoid(proj(xb, _GB0, IN_COLS)) * _dot(pm_s[...], wb_ref[...])

    def b_gates():
        xb = xb_s[...]
        gate = proj(xb, _G0, _V0)
        bv["gate"] = gate * _sigmoid(gate)
        bv["ga"] = _sigmoid(proj(xb, _GA0, _GB0))

    def b_merge():
        ya = _dot((o_s[...] * bv["gate"]).astype(BF16), wa_ref[...])
        m = bv["ga"] * ya + bv["yb"]
        z = alpha * xn_s[...] + _dot(m.astype(BF16), wo_ref[...])
        y_ref[...] = _layer_norm(z, l1g_ref[...], l1b_ref[...]).reshape(G, C, D_MODEL)

    if skew:
        order = (b_scores, a_norm, a_gates, b_pool, a_values, b_state, a_cumsum, b_gates, b_out, a_decay,
                 b_merge, a_store)
    else:
        order = (a_norm, a_gates, a_values, a_cumsum, a_decay, a_store,
                 b_scores, b_pool, b_state, b_gates, b_out, b_merge)
    for part in order:
        part()

    @pl.when(step == nsteps - 1)
    def _():
        for g, h in pairs:
            sout_ref[g, h] = st_s[g, h].T


def _ffn_kernel(x_ref, p_ref, wup_ref, wdn_ref, wpg_ref, wpp_ref, g_ref, b_ref, y_ref, *, d_ff, alpha):
    x = x_ref[...]
    xb = x.astype(BF16)
    gt = _dot(xb, wup_ref[:, :d_ff])
    up = _dot(xb, wup_ref[:, d_ff:])
    hid = (gt * _sigmoid(gt) * up).astype(BF16)
    ffn = _dot(hid, wdn_ref[...])
    ple = _sigmoid(_dot(xb, wpg_ref[...])) * _dot(p_ref[...].astype(BF16), wpp_ref[...])
    y_ref[...] = _layer_norm(alpha * x + ffn + ple, g_ref[...], b_ref[...])


def _const_spec(shape):
    zeros = (0,) * len(shape)
    return pl.BlockSpec(shape, lambda *_: zeros, pipeline_mode=pl.Buffered(1))


def _mixer(x, s0, pool0, consts, *, G, C, offset, layer, apply_ln_in, alpha):
    nseq, T, _ = x.shape
    assert nseq % G == 0 and T % C == 0 and C % 16 == 0
    R = G * C
    n_blocks = T // C
    skew = n_blocks > 1
    if skew:
        x_map = lambda i, s: (i, jnp.minimum(s, n_blocks - 1), 0)
        y_map = lambda i, s: (i, jnp.maximum(s - 1, 0), 0)
    else:
        x_map = y_map = lambda i, s: (i, s, 0)
    grid = (nseq // G, n_blocks + 1 if skew else n_blocks)
    kern = functools.partial(_mixer_kernel, G=G, C=C, offset=offset, layer=layer,
                             apply_ln_in=apply_ln_in, alpha=alpha, skew=skew)
    in_specs = [
        pl.BlockSpec((G, C, D_MODEL), x_map),
        pl.BlockSpec((G, HG_HEADS, HG_DK, HG_DV), lambda i, s: (i, 0, 0, 0)),
        pl.BlockSpec((G, POOL_CARRY, POOL_WIDTH), lambda i, s: (i, 0, 0)),
    ] + [_const_spec(c.shape) for c in consts]
    out_specs = [
        pl.BlockSpec((G, C, D_MODEL), y_map),
        pl.BlockSpec((G, HG_HEADS, HG_DK, HG_DV), lambda i, s: (i, 0, 0, 0)),
        pl.BlockSpec((G, POOL_CARRY, POOL_WIDTH), lambda i, s: (i, 0, 0)),
    ]
    out_shape = [
        jax.ShapeDtypeStruct((nseq, T, D_MODEL), F32),
        jax.ShapeDtypeStruct((nseq, HG_HEADS, HG_DK, HG_DV), F32),
        jax.ShapeDtypeStruct((nseq, POOL_CARRY, POOL_WIDTH), F32),
    ]
    decay_rows = -(-G // SUBLANES) * SUBLANES
    scratch = [
        pltpu.VMEM((R, D_MODEL), F32),
        pltpu.VMEM((R, D_MODEL), BF16),
        pltpu.VMEM((R, HG_WIDTH), BF16),
        pltpu.VMEM((R, HG_WIDTH), BF16),
        pltpu.VMEM((R, HG_WIDTH), BF16),
        pltpu.VMEM((R, HG_WIDTH), BF16),
        pltpu.VMEM((decay_rows, HG_WIDTH), F32),
        pltpu.VMEM((decay_rows, HG_WIDTH), F32),
        pltpu.VMEM((R, HG_WIDTH), F32),
        pltpu.VMEM((G, POOL_CARRY, POOL_WIDTH), F32),
        pltpu.VMEM((G, HG_HEADS, HG_DV, HG_DK), F32),
        pltpu.VMEM((R, POOL_WIDTH), BF16),
    ]
    return pl.pallas_call(
        kern,
        grid=grid,
        in_specs=in_specs,
        out_specs=out_specs,
        out_shape=out_shape,
        scratch_shapes=scratch,
        compiler_params=pltpu.CompilerParams(
            dimension_semantics=("parallel", "arbitrary"), vmem_limit_bytes=VMEM_LIMIT_BYTES),
        name="mixer",
    )(x, s0, pool0, *consts)


def _ffn(x, p, consts, *, TM, alpha):
    n, _ = x.shape
    assert n % TM == 0
    d_ff = consts[1].shape[0]
    kern = functools.partial(_ffn_kernel, d_ff=d_ff, alpha=alpha)
    return pl.pallas_call(
        kern,
        grid=(n // TM,),
        in_specs=[pl.BlockSpec((TM, D_MODEL), lambda i: (i, 0)),
                  pl.BlockSpec((TM, p.shape[1]), lambda i: (i, 0))] + [_const_spec(c.shape) for c in consts],
        out_specs=pl.BlockSpec((TM, D_MODEL), lambda i: (i, 0)),
        out_shape=jax.ShapeDtypeStruct((n, D_MODEL), F32),
        compiler_params=pltpu.CompilerParams(
            dimension_semantics=("parallel",), vmem_limit_bytes=VMEM_LIMIT_BYTES),
        name="ffn",
    )(x, p, *consts)


def kernel(x_prompt, x_sample, p_prompt, p_sample, state_hgrn, state_pool, ln_in_g, ln_in_b, lb_logits, w_in,
           hgrn_norm_g, w_branch_a, w_pool_mix, pool_scale, w_branch_b, w_out, ln1_g, ln1_b, w_ffn_up,
           w_ffn_down, w_ple_proj, w_ple_gate, ln2_g, ln2_b):
    depth = w_in.shape[0]
    alpha = float((2 * depth) ** 0.25)
    bp, tp, _ = x_prompt.shape
    bs, ts, _ = x_sample.shape
    cp, cs = min(tp, CHUNK), min(ts, CHUNK)
    gp, gs = _seqs_per_block(bp, cp), _seqs_per_block(bs, cs)
    row = lambda a: a.reshape(1, -1).astype(F32)

    xp, xs = x_prompt, x_sample
    hp, pp, hs_, ps = [], [], [], []
    for i in range(depth):
        mixer_consts = (row(ln_in_g), row(ln_in_b), lb_logits.astype(F32), w_in[i].astype(BF16),
                        row(hgrn_norm_g[i]), w_branch_a[i].astype(BF16), w_pool_mix[i].astype(BF16),
                        row(pool_scale[i]), w_branch_b[i].astype(BF16), w_out[i].astype(BF16),
                        row(ln1_g[i]), row(ln1_b[i]))
        ffn_consts = (w_ffn_up[i].astype(BF16), w_ffn_down[i].astype(BF16), w_ple_gate[i].astype(BF16),
                      w_ple_proj[i].astype(BF16), row(ln2_g[i]), row(ln2_b[i]))
        s0p = jnp.zeros((bp, HG_HEADS, HG_DK, HG_DV), F32)
        pool0p = jnp.zeros((bp, POOL_CARRY, POOL_WIDTH), F32)
        pool0s = jnp.pad(state_pool[i].astype(F32), ((0, 0), (POOL_CARRY - POOL_BUF, 0), (0, 0)))

        xp, sp, poolp = _mixer(xp, s0p, pool0p, mixer_consts, G=gp, C=cp, offset=0, layer=i,
                               apply_ln_in=(i == 0), alpha=alpha)
        xs, ss, pools = _mixer(xs, state_hgrn[i].astype(F32), pool0s, mixer_consts, G=gs, C=cs,
                               offset=PAST_LEN, layer=i, apply_ln_in=(i == 0), alpha=alpha)
        xp = _ffn(xp.reshape(bp * tp, D_MODEL), p_prompt[i].reshape(bp * tp, -1), ffn_consts,
                  TM=min(FFN_ROWS, bp * tp), alpha=alpha).reshape(bp, tp, D_MODEL)
        xs = _ffn(xs.reshape(bs * ts, D_MODEL), p_sample[i].reshape(bs * ts, -1), ffn_consts,
                  TM=min(FFN_ROWS, bs * ts), alpha=alpha).reshape(bs, ts, D_MODEL)
        hp.append(sp)
        pp.append(poolp[:, POOL_CARRY - POOL_BUF:])
        hs_.append(ss)
        ps.append(pools[:, POOL_CARRY - POOL_BUF:])
    return (xp, xs, jnp.stack(hp, axis=0), jnp.stack(pp, axis=0), jnp.stack(hs_, axis=0), jnp.stack(ps, axis=0))
```

```python
import functools

import jax
import jax.numpy as jnp
from jax import lax
from jax.experimental import pallas as pl
from jax.experimental.pallas import tpu as pltpu

F32 = jnp.float32
BF16 = jnp.bfloat16

D_MODEL = 1024
HG_HEADS = 8
HG_DK = 128
HG_DV = 128
HG_WIDTH = HG_HEADS * HG_DV
POOL_WINDOWS = (2, 4, 8, 16)
POOL_GC = 128
POOL_WIDTH = len(POOL_WINDOWS) * POOL_GC
POOL_BUF = 15
POOL_CARRY = 16
LN_EPS = 1e-5
RMS_EPS = 1e-6
EXP_CLAMP = 80.0
VMEM_LIMIT_BYTES = 56 * 1024 * 1024
CHUNK = 64
PAST_LEN = 1024
MIXER_ROWS = 256
MIXER_CHUNKS = 2
FFN_ROWS = 1024
FFN_SUB_ROWS = 256


def _seqs_per_block(nseq, chunk):
    g = max(1, min(nseq, MIXER_ROWS // chunk))
    while nseq % g:
        g -= 1
    return g


_Q0, _F0, _I0, _G0 = 0, HG_WIDTH, 2 * HG_WIDTH, 3 * HG_WIDTH
_V0 = 4 * HG_WIDTH
_GA0 = _V0 + POOL_WIDTH
_GB0 = _GA0 + D_MODEL
IN_COLS = _GB0 + D_MODEL


def _sigmoid(x):
    return 0.5 * jnp.tanh(0.5 * x) + 0.5


def _layer_norm(x, g, b):
    mu = jnp.mean(x, axis=-1, keepdims=True)
    xc = x - mu
    var = jnp.mean(xc * xc, axis=-1, keepdims=True)
    return xc * lax.rsqrt(var + LN_EPS) * g + b


def _dot(a, b):
    return jnp.dot(a, b, preferred_element_type=F32)


def _mixer_kernel(x_ref, s0_ref, pool0_ref, lng_ref, lnb_ref, lbl_ref, win_ref, hgg_ref, wa_ref,
                  wpm_ref, psc_ref, wb_ref, wo_ref, l1g_ref, l1b_ref,
                  y_ref, sout_ref, poolout_ref,
                  b_s, qd_s, kdm_s, kde_s, iv_s, o_s, pm_s, hist_s, st_s,
                  *, G, TT, C, offset, layer, apply_ln_in, alpha):
    tb = pl.program_id(1)
    ntb = pl.num_programs(1)
    R = G * C
    n_chunks = TT // C
    pairs = [(g, h) for g in range(G) for h in range(HG_HEADS)]
    hs_of = lambda h: pl.ds(h * HG_DK, HG_DK)
    rows_of = lambda g: pl.ds(g * C, C)
    nt_dims = (((1,), (1,)), ((), ()))
    tn_dims = (((0,), (0,)), ((), ()))

    @pl.when(tb == 0)
    def _():
        for g, h in pairs:
            st_s[g, h] = s0_ref[g, h].T
        for g in range(G):
            hist_s[g] = pool0_ref[g]

    lbl = lbl_ref[...]
    lbe = jnp.exp(lbl - jnp.max(lbl, axis=0, keepdims=True))
    lb = jnp.sum(lbe[0:layer + 1], axis=0, keepdims=True) / jnp.sum(lbe, axis=0, keepdims=True)

    rr = lax.broadcasted_iota(jnp.int32, (R, R), 0)
    cc = lax.broadcasted_iota(jnp.int32, (R, R), 1)
    shift = C.bit_length() - 1
    tri_blk = jnp.where(((rr >> shift) == (cc >> shift)) & (cc <= rr), 1.0, 0.0).astype(BF16)
    ti = lax.broadcasted_iota(jnp.int32, (C, C), 0)
    si = lax.broadcasted_iota(jnp.int32, (C, C), 1)
    causal = si <= ti

    for ch in range(n_chunks):
        slot = ch % b_s.shape[0]
        b_c, qd_c, kdm_c, kde_c, iv_c = b_s.at[slot], qd_s.at[slot], kdm_s.at[slot], kde_s.at[slot], iv_s.at[slot]
        o_c, pm_c = o_s.at[slot], pm_s.at[slot]
        trows = pl.ds(ch * C, C)

        x = x_ref[:, trows, :].reshape(R, D_MODEL)
        xn = _layer_norm(x, lng_ref[...], lnb_ref[...]) if apply_ln_in else x
        xb = xn.astype(BF16)

        def proj(lo, hi):
            return _dot(xb, win_ref[:, lo:hi])

        q = proj(_Q0, _F0)
        f = lb + (1.0 - lb) * _sigmoid(proj(_F0, _I0))
        logf = jnp.log(f)
        k = 1.0 - f
        iv_c[...] = proj(_I0, _G0).astype(BF16)

        logf_hi = logf.astype(BF16)
        logf_lo = (logf - logf_hi.astype(F32)).astype(BF16)
        b = _dot(tri_blk, logf_hi) + _dot(tri_blk, logf_lo)
        b_c[...] = b

        def seq_rows(row_in_chunk):
            return jnp.concatenate(
                [jnp.broadcast_to(b_c[pl.ds(g * C + row_in_chunk, 1), :], (C, HG_WIDTH)) for g in range(G)], axis=0)

        bm = seq_rows(C // 2 - 1)
        bc = seq_rows(C - 1)
        qd_c[...] = (q * jnp.exp(jnp.minimum(b - bm, EXP_CLAMP))).astype(BF16)
        kdm_c[...] = (k * jnp.exp(jnp.minimum(bm - b, EXP_CLAMP))).astype(BF16)
        kde_c[...] = (k * jnp.exp(bc - b)).astype(BF16)

        res = {}
        for g, h in pairs:
            rows, hs = rows_of(g), hs_of(h)
            em = jnp.exp(b_c[pl.ds(g * C + C // 2 - 1, 1), hs])
            w = jnp.concatenate([(st_s[g, h] * em).astype(BF16), kdm_c[rows, hs]], axis=0)
            res[g, h] = lax.dot_general(qd_c[rows, hs], w, nt_dims, preferred_element_type=F32)

        v = proj(_V0, _GA0)
        pos1 = (offset + 1 + tb * TT + ch * C + lax.broadcasted_iota(jnp.int32, (C, 1), 0)).astype(F32)
        wins = [[] for _ in POOL_WINDOWS]
        for g in range(G):
            ext = jnp.concatenate([hist_s[g], v[g * C:(g + 1) * C]], axis=0)
            tail = ext[C:C + POOL_CARRY]
            if ch == n_chunks - 1:
                poolout_ref[g] = tail
            hist_s[g] = tail
            acc = ext
            for j, w_len in enumerate(POOL_WINDOWS):
                acc = acc + pltpu.roll(acc, w_len // 2, 0)
                wins[j].append(acc[POOL_CARRY:, :POOL_GC])
                acc = acc[:, POOL_GC:]
        for j, w_len in enumerate(POOL_WINDOWS):
            js = pl.ds(j * POOL_GC, POOL_GC)
            win = jnp.concatenate(wins[j], axis=0) if G > 1 else wins[j][0]
            cnt = jnp.minimum(pos1, float(w_len))
            cnt = jnp.concatenate([cnt] * G, axis=0) if G > 1 else cnt
            pooled = win / cnt - v[:, j * POOL_GC:(j + 1) * POOL_GC]
            pm_c[:, js] = (_dot(pooled.astype(BF16), wpm_ref[j]) * psc_ref[:, js]).astype(BF16)
        yb = _sigmoid(proj(_GB0, IN_COLS)) * _dot(pm_c[...], wb_ref[...])

        for g, h in pairs:
            rows, hs = rows_of(g), hs_of(h)
            ec = jnp.exp(b_c[pl.ds(g * C + C - 1, 1), hs])
            st_s[g, h] = st_s[g, h] * ec + lax.dot_general(iv_c[rows, hs], kde_c[rows, hs], tn_dims,
                                                           preferred_element_type=F32)
        gate = proj(_G0, _V0)
        gate = gate * _sigmoid(gate)
        ga = _sigmoid(proj(_GA0, _GB0))
        for g, h in pairs:
            rows, hs = rows_of(g), hs_of(h)
            p = jnp.where(causal, res[g, h][:, HG_DV:], 0.0).astype(BF16)
            o = _dot(p, iv_c[rows, hs]) + res[g, h][:, :HG_DV]
            ms = jnp.mean(o * o, axis=-1, keepdims=True)
            o_c[rows, hs] = o * lax.rsqrt(ms + RMS_EPS) * hgg_ref[:, hs]

        ya = _dot((o_c[...] * gate).astype(BF16), wa_ref[...])
        m = ga * ya + yb
        z = alpha * xn + _dot(m.astype(BF16), wo_ref[...])
        y_ref[:, trows, :] = _layer_norm(z, l1g_ref[...], l1b_ref[...]).reshape(G, C, D_MODEL)

    @pl.when(tb == ntb - 1)
    def _():
        for g, h in pairs:
            sout_ref[g, h] = st_s[g, h].T


def _ffn_kernel(x_ref, p_ref, wup_ref, wdn_ref, wpg_ref, wpp_ref, g_ref, b_ref, y_ref, *, d_ff, alpha):
    for r0 in range(0, x_ref.shape[0], FFN_SUB_ROWS):
        rows = pl.ds(r0, FFN_SUB_ROWS)
        x = x_ref[rows, :]
        xb = x.astype(BF16)
        gt = _dot(xb, wup_ref[:, :d_ff])
        up = _dot(xb, wup_ref[:, d_ff:])
        hid = (gt * _sigmoid(gt) * up).astype(BF16)
        ffn = _dot(hid, wdn_ref[...])
        ple = _sigmoid(_dot(xb, wpg_ref[...])) * _dot(p_ref[rows, :].astype(BF16), wpp_ref[...])
        y_ref[rows, :] = _layer_norm(alpha * x + ffn + ple, g_ref[...], b_ref[...])


def _const_spec(shape):
    zeros = (0,) * len(shape)
    return pl.BlockSpec(shape, lambda *_: zeros, pipeline_mode=pl.Buffered(1))


def _mixer(x, s0, pool0, consts, *, G, TT, C, offset, layer, apply_ln_in, alpha):
    nseq, T, _ = x.shape
    assert nseq % G == 0 and T % TT == 0 and TT % C == 0 and C % 16 == 0
    grid = (nseq // G, T // TT)
    kern = functools.partial(_mixer_kernel, G=G, TT=TT, C=C, offset=offset, layer=layer,
                             apply_ln_in=apply_ln_in, alpha=alpha)
    in_specs = [
        pl.BlockSpec((G, TT, D_MODEL), lambda i, t: (i, t, 0)),
        pl.BlockSpec((G, HG_HEADS, HG_DK, HG_DV), lambda i, t: (i, 0, 0, 0)),
        pl.BlockSpec((G, POOL_CARRY, POOL_WIDTH), lambda i, t: (i, 0, 0)),
    ] + [_const_spec(c.shape) for c in consts]
    out_specs = [
        pl.BlockSpec((G, TT, D_MODEL), lambda i, t: (i, t, 0)),
        pl.BlockSpec((G, HG_HEADS, HG_DK, HG_DV), lambda i, t: (i, 0, 0, 0)),
        pl.BlockSpec((G, POOL_CARRY, POOL_WIDTH), lambda i, t: (i, 0, 0)),
    ]
    out_shape = [
        jax.ShapeDtypeStruct((nseq, T, D_MODEL), F32),
        jax.ShapeDtypeStruct((nseq, HG_HEADS, HG_DK, HG_DV), F32),
        jax.ShapeDtypeStruct((nseq, POOL_CARRY, POOL_WIDTH), F32),
    ]
    rows = G * C
    slots = min(2, TT // C)
    scratch = [
        pltpu.VMEM((slots, rows, HG_WIDTH), F32),
        pltpu.VMEM((slots, rows, HG_WIDTH), BF16),
        pltpu.VMEM((slots, rows, HG_WIDTH), BF16),
        pltpu.VMEM((slots, rows, HG_WIDTH), BF16),
        pltpu.VMEM((slots, rows, HG_WIDTH), BF16),
        pltpu.VMEM((slots, rows, HG_WIDTH), F32),
        pltpu.VMEM((slots, rows, POOL_WIDTH), BF16),
        pltpu.VMEM((G, POOL_CARRY, POOL_WIDTH), F32),
        pltpu.VMEM((G, HG_HEADS, HG_DV, HG_DK), F32),
    ]
    return pl.pallas_call(
        kern,
        grid=grid,
        in_specs=in_specs,
        out_specs=out_specs,
        out_shape=out_shape,
        scratch_shapes=scratch,
        compiler_params=pltpu.CompilerParams(
            dimension_semantics=("parallel", "arbitrary"), vmem_limit_bytes=VMEM_LIMIT_BYTES),
        name="mixer",
    )(x, s0, pool0, *consts)


def _ffn(x, p, consts, *, TM, alpha):
    n, _ = x.shape
    assert n % TM == 0 and TM % FFN_SUB_ROWS == 0
    d_ff = consts[1].shape[0]
    kern = functools.partial(_ffn_kernel, d_ff=d_ff, alpha=alpha)
    return pl.pallas_call(
        kern,
        grid=(n // TM,),
        in_specs=[pl.BlockSpec((TM, D_MODEL), lambda i: (i, 0)),
                  pl.BlockSpec((TM, p.shape[1]), lambda i: (i, 0))] + [_const_spec(c.shape) for c in consts],
        out_specs=pl.BlockSpec((TM, D_MODEL), lambda i: (i, 0)),
        out_shape=jax.ShapeDtypeStruct((n, D_MODEL), F32),
        compiler_params=pltpu.CompilerParams(
            dimension_semantics=("parallel",), vmem_limit_bytes=VMEM_LIMIT_BYTES),
        name="ffn",
    )(x, p, *consts)


def kernel(x_prompt, x_sample, p_prompt, p_sample, state_hgrn, state_pool, ln_in_g, ln_in_b, lb_logits, w_in,
           hgrn_norm_g, w_branch_a, w_pool_mix, pool_scale, w_branch_b, w_out, ln1_g, ln1_b, w_ffn_up,
           w_ffn_down, w_ple_proj, w_ple_gate, ln2_g, ln2_b):
    depth = w_in.shape[0]
    alpha = float((2 * depth) ** 0.25)
    bp, tp, _ = x_prompt.shape
    bs, ts, _ = x_sample.shape
    cp, cs = min(tp, CHUNK), min(ts, CHUNK)
    gp, gs = _seqs_per_block(bp, cp), _seqs_per_block(bs, cs)
    ttp, tts = cp * min(MIXER_CHUNKS, tp // cp), cs * min(MIXER_CHUNKS, ts // cs)
    row = lambda a: a.reshape(1, -1).astype(F32)

    xp, xs = x_prompt, x_sample
    hp, pp, hs_, ps = [], [], [], []
    for i in range(depth):
        mixer_consts = (row(ln_in_g), row(ln_in_b), lb_logits.astype(F32), w_in[i].astype(BF16),
                        row(hgrn_norm_g[i]), w_branch_a[i].astype(BF16), w_pool_mix[i].astype(BF16),
                        row(pool_scale[i]), w_branch_b[i].astype(BF16), w_out[i].astype(BF16),
                        row(ln1_g[i]), row(ln1_b[i]))
        ffn_consts = (w_ffn_up[i].astype(BF16), w_ffn_down[i].astype(BF16), w_ple_gate[i].astype(BF16),
                      w_ple_proj[i].astype(BF16), row(ln2_g[i]), row(ln2_b[i]))
        s0p = jnp.zeros((bp, HG_HEADS, HG_DK, HG_DV), F32)
        pool0p = jnp.zeros((bp, POOL_CARRY, POOL_WIDTH), F32)
        pool0s = jnp.pad(state_pool[i].astype(F32), ((0, 0), (POOL_CARRY - POOL_BUF, 0), (0, 0)))

        xp, sp, poolp = _mixer(xp, s0p, pool0p, mixer_consts, G=gp, TT=ttp, C=cp, offset=0, layer=i,
                               apply_ln_in=(i == 0), alpha=alpha)
        xs, ss, pools = _mixer(xs, state_hgrn[i].astype(F32), pool0s, mixer_consts, G=gs, TT=tts, C=cs,
                               offset=PAST_LEN, layer=i, apply_ln_in=(i == 0), alpha=alpha)
        xp = _ffn(xp.reshape(bp * tp, D_MODEL), p_prompt[i].reshape(bp * tp, -1), ffn_consts,
                  TM=min(FFN_ROWS, bp * tp), alpha=alpha).reshape(bp, tp, D_MODEL)
        xs = _ffn(xs.reshape(bs * ts, D_MODEL), p_sample[i].reshape(bs * ts, -1), ffn_consts,
                  TM=min(FFN_ROWS, bs * ts), alpha=alpha).reshape(bs, ts, D_MODEL)
        hp.append(sp)
        pp.append(poolp[:, POOL_CARRY - POOL_BUF:])
        hs_.append(ss)
        ps.append(pools[:, POOL_CARRY - POOL_BUF:])
    return (xp, xs, jnp.stack(hp, axis=0), jnp.stack(pp, axis=0), jnp.stack(hs_, axis=0), jnp.stack(ps, axis=0))
```

```python
import functools

import jax
import jax.numpy as jnp
from jax import lax
from jax.experimental import pallas as pl
from jax.experimental.pallas import tpu as pltpu

F32 = jnp.float32
BF16 = jnp.bfloat16

D_MODEL = 1024
HG_HEADS = 8
HG_DK = 128
HG_DV = 128
HG_WIDTH = HG_HEADS * HG_DV
POOL_WINDOWS = (2, 4, 8, 16)
POOL_GC = 128
POOL_WIDTH = len(POOL_WINDOWS) * POOL_GC
POOL_BUF = 15
POOL_CARRY = 16
LN_EPS = 1e-5
RMS_EPS = 1e-6
EXP_CLAMP = 80.0
VMEM_LIMIT_BYTES = 56 * 1024 * 1024
CHUNK = 64
PAST_LEN = 1024
MIXER_ROWS = 256
MIXER_CHUNKS = 2
FFN_ROWS = 1024
FFN_SUB_ROWS = 256


def _seqs_per_block(nseq, chunk):
    g = max(1, min(nseq, MIXER_ROWS // chunk))
    while nseq % g:
        g -= 1
    return g


_Q0, _F0, _I0, _G0 = 0, HG_WIDTH, 2 * HG_WIDTH, 3 * HG_WIDTH
_V0 = 4 * HG_WIDTH
_GA0 = _V0 + POOL_WIDTH
_GB0 = _GA0 + D_MODEL
IN_COLS = _GB0 + D_MODEL


def _sigmoid(x):
    return 0.5 * jnp.tanh(0.5 * x) + 0.5


def _layer_norm(x, g, b):
    mu = jnp.mean(x, axis=-1, keepdims=True)
    xc = x - mu
    var = jnp.mean(xc * xc, axis=-1, keepdims=True)
    return xc * lax.rsqrt(var + LN_EPS) * g + b


def _dot(a, b):
    return jnp.dot(a, b, preferred_element_type=F32)


def _mixer_kernel(x_ref, s0_ref, pool0_ref, lng_ref, lnb_ref, lbl_ref, win_ref, hgg_ref, wa_ref,
                  wpm_ref, psc_ref, wb_ref, wo_ref, l1g_ref, l1b_ref,
                  y_ref, sout_ref, poolout_ref,
                  b_s, qd_s, kdm_s, kde_s, iv_s, o_s, pm_s, hist_s, st_s,
                  *, G, TT, C, offset, layer, apply_ln_in, alpha):
    tb = pl.program_id(1)
    ntb = pl.num_programs(1)
    R = G * C
    n_chunks = TT // C
    pairs = [(g, h) for g in range(G) for h in range(HG_HEADS)]
    hs_of = lambda h: pl.ds(h * HG_DK, HG_DK)
    rows_of = lambda g: pl.ds(g * C, C)
    nt_dims = (((1,), (1,)), ((), ()))
    tn_dims = (((0,), (0,)), ((), ()))

    @pl.when(tb == 0)
    def _():
        for g, h in pairs:
            st_s[g, h] = s0_ref[g, h].T
        for g in range(G):
            hist_s[g] = pool0_ref[g]

    lbl = lbl_ref[...]
    lbe = jnp.exp(lbl - jnp.max(lbl, axis=0, keepdims=True))
    lb = jnp.sum(lbe[0:layer + 1], axis=0, keepdims=True) / jnp.sum(lbe, axis=0, keepdims=True)

    rr = lax.broadcasted_iota(jnp.int32, (R, R), 0)
    cc = lax.broadcasted_iota(jnp.int32, (R, R), 1)
    shift = C.bit_length() - 1
    tri_blk = jnp.where(((rr >> shift) == (cc >> shift)) & (cc <= rr), 1.0, 0.0).astype(BF16)
    ti = lax.broadcasted_iota(jnp.int32, (C, C), 0)
    si = lax.broadcasted_iota(jnp.int32, (C, C), 1)
    causal = si <= ti

    for ch in range(n_chunks):
        slot = ch % b_s.shape[0]
        b_c, qd_c, kdm_c, kde_c, iv_c = b_s.at[slot], qd_s.at[slot], kdm_s.at[slot], kde_s.at[slot], iv_s.at[slot]
        o_c, pm_c = o_s.at[slot], pm_s.at[slot]
        trows = pl.ds(ch * C, C)

        x = x_ref[:, trows, :].reshape(R, D_MODEL)
        xn = _layer_norm(x, lng_ref[...], lnb_ref[...]) if apply_ln_in else x
        xb = xn.astype(BF16)

        def proj(lo, hi):
            return _dot(xb, win_ref[:, lo:hi])

        q = proj(_Q0, _F0)
        f = lb + (1.0 - lb) * _sigmoid(proj(_F0, _I0))
        logf = jnp.log(f)
        k = 1.0 - f
        iv_c[...] = proj(_I0, _G0).astype(BF16)

        logf_hi = logf.astype(BF16)
        logf_lo = (logf - logf_hi.astype(F32)).astype(BF16)
        b = _dot(tri_blk, logf_hi) + _dot(tri_blk, logf_lo)
        b_c[...] = b

        bc = jnp.concatenate(
            [jnp.broadcast_to(b_c[pl.ds(g * C + C - 1, 1), :], (C, HG_WIDTH)) for g in range(G)], axis=0)
        bm = 0.5 * bc
        qd_c[...] = (q * jnp.exp(jnp.minimum(b - bm, EXP_CLAMP))).astype(BF16)
        kdm_c[...] = (k * jnp.exp(jnp.minimum(bm - b, EXP_CLAMP))).astype(BF16)
        kde_c[...] = (k * jnp.exp(bc - b)).astype(BF16)

        res = {}
        for g, h in pairs:
            rows, hs = rows_of(g), hs_of(h)
            em = jnp.exp(0.5 * b_c[pl.ds(g * C + C - 1, 1), hs])
            w = jnp.concatenate([(st_s[g, h] * em).astype(BF16), kdm_c[rows, hs]], axis=0)
            res[g, h] = lax.dot_general(qd_c[rows, hs], w, nt_dims, preferred_element_type=F32)

        v = proj(_V0, _GA0)
        pos1 = (offset + 1 + tb * TT + ch * C + lax.broadcasted_iota(jnp.int32, (C, 1), 0)).astype(F32)
        wins = [[] for _ in POOL_WINDOWS]
        for g in range(G):
            ext = jnp.concatenate([hist_s[g], v[g * C:(g + 1) * C]], axis=0)
            tail = ext[C:C + POOL_CARRY]
            if ch == n_chunks - 1:
                poolout_ref[g] = tail
            hist_s[g] = tail
            acc = ext
            for j, w_len in enumerate(POOL_WINDOWS):
                acc = acc + pltpu.roll(acc, w_len // 2, 0)
                wins[j].append(acc[POOL_CARRY:, :POOL_GC])
                acc = acc[:, POOL_GC:]
        for j, w_len in enumerate(POOL_WINDOWS):
            js = pl.ds(j * POOL_GC, POOL_GC)
            win = jnp.concatenate(wins[j], axis=0) if G > 1 else wins[j][0]
            cnt = jnp.minimum(pos1, float(w_len))
            cnt = jnp.concatenate([cnt] * G, axis=0) if G > 1 else cnt
            pooled = win / cnt - v[:, j * POOL_GC:(j + 1) * POOL_GC]
            pm_c[:, js] = (_dot(pooled.astype(BF16), wpm_ref[j]) * psc_ref[:, js]).astype(BF16)
        yb = _sigmoid(proj(_GB0, IN_COLS)) * _dot(pm_c[...], wb_ref[...])

        for g, h in pairs:
            rows, hs = rows_of(g), hs_of(h)
            ec = jnp.exp(b_c[pl.ds(g * C + C - 1, 1), hs])
            st_s[g, h] = st_s[g, h] * ec + lax.dot_general(iv_c[rows, hs], kde_c[rows, hs], tn_dims,
                                                           preferred_element_type=F32)
        gate = proj(_G0, _V0)
        gate = gate * _sigmoid(gate)
        ga = _sigmoid(proj(_GA0, _GB0))
        for g, h in pairs:
            rows, hs = rows_of(g), hs_of(h)
            p = jnp.where(causal, res[g, h][:, HG_DV:], 0.0).astype(BF16)
            o = _dot(p, iv_c[rows, hs]) + res[g, h][:, :HG_DV]
            ms = jnp.mean(o * o, axis=-1, keepdims=True)
            o_c[rows, hs] = o * lax.rsqrt(ms + RMS_EPS) * hgg_ref[:, hs]

        ya = _dot((o_c[...] * gate).astype(BF16), wa_ref[...])
        m = ga * ya + yb
        z = alpha * xn + _dot(m.astype(BF16), wo_ref[...])
        y_ref[:, trows, :] = _layer_norm(z, l1g_ref[...], l1b_ref[...]).reshape(G, C, D_MODEL)

    @pl.when(tb == ntb - 1)
    def _():
        for g, h in pairs:
            sout_ref[g, h] = st_s[g, h].T


def _ffn_kernel(x_ref, p_ref, wup_ref, wdn_ref, wpg_ref, wpp_ref, g_ref, b_ref, y_ref, *, d_ff, alpha):
    for r0 in range(0, x_ref.shape[0], FFN_SUB_ROWS):
        rows = pl.ds(r0, FFN_SUB_ROWS)
        x = x_ref[rows, :]
        xb = x.astype(BF16)
        gt = _dot(xb, wup_ref[:, :d_ff])
        up = _dot(xb, wup_ref[:, d_ff:])
        hid = (gt * _sigmoid(gt) * up).astype(BF16)
        ffn = _dot(hid, wdn_ref[...])
        ple = _sigmoid(_dot(xb, wpg_ref[...])) * _dot(p_ref[rows, :].astype(BF16), wpp_ref[...])
        y_ref[rows, :] = _layer_norm(alpha * x + ffn + ple, g_ref[...], b_ref[...])


def _const_spec(shape):
    zeros = (0,) * len(shape)
    return pl.BlockSpec(shape, lambda *_: zeros, pipeline_mode=pl.Buffered(1))


def _mixer(x, s0, pool0, consts, *, G, TT, C, offset, layer, apply_ln_in, alpha):
    nseq, T, _ = x.shape
    assert nseq % G == 0 and T % TT == 0 and TT % C == 0 and C % 16 == 0
    grid = (nseq // G, T // TT)
    kern = functools.partial(_mixer_kernel, G=G, TT=TT, C=C, offset=offset, layer=layer,
                             apply_ln_in=apply_ln_in, alpha=alpha)
    in_specs = [
        pl.BlockSpec((G, TT, D_MODEL), lambda i, t: (i, t, 0)),
        pl.BlockSpec((G, HG_HEADS, HG_DK, HG_DV), lambda i, t: (i, 0, 0, 0)),
        pl.BlockSpec((G, POOL_CARRY, POOL_WIDTH), lambda i, t: (i, 0, 0)),
    ] + [_const_spec(c.shape) for c in consts]
    out_specs = [
        pl.BlockSpec((G, TT, D_MODEL), lambda i, t: (i, t, 0)),
        pl.BlockSpec((G, HG_HEADS, HG_DK, HG_DV), lambda i, t: (i, 0, 0, 0)),
        pl.BlockSpec((G, POOL_CARRY, POOL_WIDTH), lambda i, t: (i, 0, 0)),
    ]
    out_shape = [
        jax.ShapeDtypeStruct((nseq, T, D_MODEL), F32),
        jax.ShapeDtypeStruct((nseq, HG_HEADS, HG_DK, HG_DV), F32),
        jax.ShapeDtypeStruct((nseq, POOL_CARRY, POOL_WIDTH), F32),
    ]
    rows = G * C
    slots = min(2, TT // C)
    scratch = [
        pltpu.VMEM((slots, rows, HG_WIDTH), F32),
        pltpu.VMEM((slots, rows, HG_WIDTH), BF16),
        pltpu.VMEM((slots, rows, HG_WIDTH), BF16),
        pltpu.VMEM((slots, rows, HG_WIDTH), BF16),
        pltpu.VMEM((slots, rows, HG_WIDTH), BF16),
        pltpu.VMEM((slots, rows, HG_WIDTH), F32),
        pltpu.VMEM((slots, rows, POOL_WIDTH), BF16),
        pltpu.VMEM((G, POOL_CARRY, POOL_WIDTH), F32),
        pltpu.VMEM((G, HG_HEADS, HG_DV, HG_DK), F32),
    ]
    return pl.pallas_call(
        kern,
        grid=grid,
        in_specs=in_specs,
        out_specs=out_specs,
        out_shape=out_shape,
        scratch_shapes=scratch,
        compiler_params=pltpu.CompilerParams(
            dimension_semantics=("parallel", "arbitrary"), vmem_limit_bytes=VMEM_LIMIT_BYTES),
        name="mixer",
    )(x, s0, pool0, *consts)


def _ffn(x, p, consts, *, TM, alpha):
    n, _ = x.shape
    assert n % TM == 0 and TM % FFN_SUB_ROWS == 0
    d_ff = consts[1].shape[0]
    kern = functools.partial(_ffn_kernel, d_ff=d_ff, alpha=alpha)
    return pl.pallas_call(
        kern,
        grid=(n // TM,),
        in_specs=[pl.BlockSpec((TM, D_MODEL), lambda i: (i, 0)),
                  pl.BlockSpec((TM, p.shape[1]), lambda i: (i, 0))] + [_const_spec(c.shape) for c in consts],
        out_specs=pl.BlockSpec((TM, D_MODEL), lambda i: (i, 0)),
        out_shape=jax.ShapeDtypeStruct((n, D_MODEL), F32),
        compiler_params=pltpu.CompilerParams(
            dimension_semantics=("parallel",), vmem_limit_bytes=VMEM_LIMIT_BYTES),
        name="ffn",
    )(x, p, *consts)


def kernel(x_prompt, x_sample, p_prompt, p_sample, state_hgrn, state_pool, ln_in_g, ln_in_b, lb_logits, w_in,
           hgrn_norm_g, w_branch_a, w_pool_mix, pool_scale, w_branch_b, w_out, ln1_g, ln1_b, w_ffn_up,
           w_ffn_down, w_ple_proj, w_ple_gate, ln2_g, ln2_b):
    depth = w_in.shape[0]
    alpha = float((2 * depth) ** 0.25)
    bp, tp, _ = x_prompt.shape
    bs, ts, _ = x_sample.shape
    cp, cs = min(tp, CHUNK), min(ts, CHUNK)
    gp, gs = _seqs_per_block(bp, cp), _seqs_per_block(bs, cs)
    ttp, tts = cp * min(MIXER_CHUNKS, tp // cp), cs * min(MIXER_CHUNKS, ts // cs)
    row = lambda a: a.reshape(1, -1).astype(F32)

    xp, xs = x_prompt, x_sample
    hp, pp, hs_, ps = [], [], [], []
    for i in range(depth):
        mixer_consts = (row(ln_in_g), row(ln_in_b), lb_logits.astype(F32), w_in[i].astype(BF16),
                        row(hgrn_norm_g[i]), w_branch_a[i].astype(BF16), w_pool_mix[i].astype(BF16),
                        row(pool_scale[i]), w_branch_b[i].astype(BF16), w_out[i].astype(BF16),
                        row(ln1_g[i]), row(ln1_b[i]))
        ffn_consts = (w_ffn_up[i].astype(BF16), w_ffn_down[i].astype(BF16), w_ple_gate[i].astype(BF16),
                      w_ple_proj[i].astype(BF16), row(ln2_g[i]), row(ln2_b[i]))
        s0p = jnp.zeros((bp, HG_HEADS, HG_DK, HG_DV), F32)
        pool0p = jnp.zeros((bp, POOL_CARRY, POOL_WIDTH), F32)
        pool0s = jnp.pad(state_pool[i].astype(F32), ((0, 0), (POOL_CARRY - POOL_BUF, 0), (0, 0)))

        xp, sp, poolp = _mixer(xp, s0p, pool0p, mixer_consts, G=gp, TT=ttp, C=cp, offset=0, layer=i,
                               apply_ln_in=(i == 0), alpha=alpha)
        xs, ss, pools = _mixer(xs, state_hgrn[i].astype(F32), pool0s, mixer_consts, G=gs, TT=tts, C=cs,
                               offset=PAST_LEN, layer=i, apply_ln_in=(i == 0), alpha=alpha)
        xp = _ffn(xp.reshape(bp * tp, D_MODEL), p_prompt[i].reshape(bp * tp, -1), ffn_consts,
                  TM=min(FFN_ROWS, bp * tp), alpha=alpha).reshape(bp, tp, D_MODEL)
        xs = _ffn(xs.reshape(bs * ts, D_MODEL), p_sample[i].reshape(bs * ts, -1), ffn_consts,
                  TM=min(FFN_ROWS, bs * ts), alpha=alpha).reshape(bs, ts, D_MODEL)
        hp.append(sp)
        pp.append(poolp[:, POOL_CARRY - POOL_BUF:])
        hs_.append(ss)
        ps.append(pools[:, POOL_CARRY - POOL_BUF:])
    return (xp, xs, jnp.stack(hp, axis=0), jnp.stack(pp, axis=0), jnp.stack(hs_, axis=0), jnp.stack(ps, axis=0))
```

```python
import functools

import jax
import jax.numpy as jnp
from jax import lax
from jax.experimental import pallas as pl
from jax.experimental.pallas import tpu as pltpu

F32 = jnp.float32
BF16 = jnp.bfloat16

D_MODEL = 1024
HG_HEADS = 8
HG_DK = 128
HG_DV = 128
HG_WIDTH = HG_HEADS * HG_DV
POOL_WINDOWS = (2, 4, 8, 16)
POOL_GC = 128
POOL_WIDTH = len(POOL_WINDOWS) * POOL_GC
POOL_BUF = 15
POOL_CARRY = 16
LN_EPS = 1e-5
RMS_EPS = 1e-6
EXP_CLAMP = 80.0
VMEM_LIMIT_BYTES = 56 * 1024 * 1024
CHUNK = 64
PAST_LEN = 1024
MIXER_ROWS = 256
MIXER_CHUNKS = 2
FFN_ROWS = 1024
FFN_SUB_ROWS = 256


def _seqs_per_block(nseq, chunk):
    g = max(1, min(nseq, MIXER_ROWS // chunk))
    while nseq % g:
        g -= 1
    return g


_Q0, _F0, _I0, _G0 = 0, HG_WIDTH, 2 * HG_WIDTH, 3 * HG_WIDTH
_V0 = 4 * HG_WIDTH
_GA0 = _V0 + POOL_WIDTH
_GB0 = _GA0 + D_MODEL
IN_COLS = _GB0 + D_MODEL


def _sigmoid(x):
    return 0.5 * jnp.tanh(0.5 * x) + 0.5


def _layer_norm(x, g, b):
    mu = jnp.mean(x, axis=-1, keepdims=True)
    xc = x - mu
    var = jnp.mean(xc * xc, axis=-1, keepdims=True)
    return xc * lax.rsqrt(var + LN_EPS) * g + b


def _dot(a, b):
    return jnp.dot(a, b, preferred_element_type=F32)


def _mixer_kernel(x_ref, s0_ref, pool0_ref, lng_ref, lnb_ref, lbl_ref, win_ref, hgg_ref, wa_ref,
                  wpm_ref, psc_ref, wb_ref, wo_ref, l1g_ref, l1b_ref,
                  y_ref, sout_ref, poolout_ref,
                  b_s, qd_s, kdm_s, kde_s, iv_s, o_s, pm_s, hist_s, st_s,
                  *, G, TT, C, offset, layer, apply_ln_in, alpha):
    tb = pl.program_id(1)
    ntb = pl.num_programs(1)
    R = G * C
    n_chunks = TT // C
    pairs = [(g, h) for g in range(G) for h in range(HG_HEADS)]
    hs_of = lambda h: pl.ds(h * HG_DK, HG_DK)
    rows_of = lambda g: pl.ds(g * C, C)
    nt_dims = (((1,), (1,)), ((), ()))
    tn_dims = (((0,), (0,)), ((), ()))

    @pl.when(tb == 0)
    def _():
        for g, h in pairs:
            st_s[g, h] = s0_ref[g, h].T
        for g in range(G):
            hist_s[g] = pool0_ref[g]

    lbl = lbl_ref[...]
    lbe = jnp.exp(lbl - jnp.max(lbl, axis=0, keepdims=True))
    lb = jnp.sum(lbe[0:layer + 1], axis=0, keepdims=True) / jnp.sum(lbe, axis=0, keepdims=True)

    rr = lax.broadcasted_iota(jnp.int32, (R, R), 0)
    cc = lax.broadcasted_iota(jnp.int32, (R, R), 1)
    shift = C.bit_length() - 1
    tri_blk = jnp.where(((rr >> shift) == (cc >> shift)) & (cc <= rr), 1.0, 0.0).astype(BF16)
    ti = lax.broadcasted_iota(jnp.int32, (C, C), 0)
    si = lax.broadcasted_iota(jnp.int32, (C, C), 1)
    causal = si <= ti

    for ch in range(n_chunks):
        slot = ch % b_s.shape[0]
        b_c, qd_c, kdm_c, kde_c, iv_c = b_s.at[slot], qd_s.at[slot], kdm_s.at[slot], kde_s.at[slot], iv_s.at[slot]
        o_c, pm_c = o_s.at[slot], pm_s.at[slot]
        trows = pl.ds(ch * C, C)

        x = x_ref[:, trows, :].reshape(R, D_MODEL)
        xn = _layer_norm(x, lng_ref[...], lnb_ref[...]) if apply_ln_in else x
        xb = xn.astype(BF16)

        def proj(lo, hi):
            return _dot(xb, win_ref[:, lo:hi])

        q = proj(_Q0, _F0)
        f = lb + (1.0 - lb) * _sigmoid(proj(_F0, _I0))
        logf = jnp.log(f)
        k = 1.0 - f
        iv_c[...] = proj(_I0, _G0).astype(BF16)

        logf_hi = logf.astype(BF16)
        logf_lo = (logf - logf_hi.astype(F32)).astype(BF16)
        b = _dot(tri_blk, logf_hi) + _dot(tri_blk, logf_lo)
        b_c[...] = b

        bc = jnp.concatenate(
            [jnp.broadcast_to(b_c[pl.ds(g * C + C - 1, 1), :], (C, HG_WIDTH)) for g in range(G)], axis=0)
        bm = 0.5 * bc
        qd_c[...] = (q * jnp.exp(jnp.minimum(b - bm, EXP_CLAMP))).astype(BF16)
        kdm_c[...] = (k * jnp.exp(jnp.minimum(bm - b, EXP_CLAMP))).astype(BF16)
        kde_c[...] = (k * jnp.exp(bc - b)).astype(BF16)

        res = {}
        for g, h in pairs:
            rows, hs = rows_of(g), hs_of(h)
            em = jnp.exp(0.5 * b_c[pl.ds(g * C + C - 1, 1), hs])
            w = jnp.concatenate([(st_s[g, h] * em).astype(BF16), kdm_c[rows, hs]], axis=0)
            res[g, h] = lax.dot_general(qd_c[rows, hs], w, nt_dims, preferred_element_type=F32)

        v = proj(_V0, _GA0)
        pos1 = (offset + 1 + tb * TT + ch * C + lax.broadcasted_iota(jnp.int32, (C, 1), 0)).astype(F32)
        wins = [[] for _ in POOL_WINDOWS]
        for g in range(G):
            ext = jnp.concatenate([hist_s[g], v[g * C:(g + 1) * C]], axis=0)
            tail = ext[C:C + POOL_CARRY]
            if ch == n_chunks - 1:
                poolout_ref[g] = tail
            hist_s[g] = tail
            acc = ext
            for j, w_len in enumerate(POOL_WINDOWS):
                acc = acc + pltpu.roll(acc, w_len // 2, 0)
                wins[j].append(acc[POOL_CARRY:, :POOL_GC])
                acc = acc[:, POOL_GC:]
        for j, w_len in enumerate(POOL_WINDOWS):
            js = pl.ds(j * POOL_GC, POOL_GC)
            win = jnp.concatenate(wins[j], axis=0) if G > 1 else wins[j][0]
            cnt = jnp.minimum(pos1, float(w_len))
            cnt = jnp.concatenate([cnt] * G, axis=0) if G > 1 else cnt
            pooled = win / cnt - v[:, j * POOL_GC:(j + 1) * POOL_GC]
            pm_c[:, js] = (_dot(pooled.astype(BF16), wpm_ref[j]) * psc_ref[:, js]).astype(BF16)
        yb = _sigmoid(proj(_GB0, IN_COLS)) * _dot(pm_c[...], wb_ref[...])

        for g, h in pairs:
            rows, hs = rows_of(g), hs_of(h)
            ec = jnp.exp(b_c[pl.ds(g * C + C - 1, 1), hs])
            st_s[g, h] = st_s[g, h] * ec + lax.dot_general(iv_c[rows, hs], kde_c[rows, hs], tn_dims,
                                                           preferred_element_type=F32)
        gate = proj(_G0, _V0)
        gate = gate * _sigmoid(gate)
        ga = _sigmoid(proj(_GA0, _GB0))
        for g, h in pairs:
            rows, hs = rows_of(g), hs_of(h)
            p = jnp.where(causal, res[g, h][:, HG_DV:], 0.0).astype(BF16)
            o = _dot(p, iv_c[rows, hs]) + res[g, h][:, :HG_DV]
            o_c[rows, hs] = o

        ones_blk = jnp.ones((HG_DV, HG_DV), BF16)
        for h in range(HG_HEADS):
            hs = hs_of(h)
            oh = o_c[:, hs]
            ss = _dot((oh * oh).astype(BF16), ones_blk)
            o_c[:, hs] = oh * lax.rsqrt(ss * (1.0 / HG_DV) + RMS_EPS) * hgg_ref[:, hs]

        ya = _dot((o_c[...] * gate).astype(BF16), wa_ref[...])
        m = ga * ya + yb
        z = alpha * xn + _dot(m.astype(BF16), wo_ref[...])
        y_ref[:, trows, :] = _layer_norm(z, l1g_ref[...], l1b_ref[...]).reshape(G, C, D_MODEL)

    @pl.when(tb == ntb - 1)
    def _():
        for g, h in pairs:
            sout_ref[g, h] = st_s[g, h].T


def _ffn_kernel(x_ref, p_ref, wup_ref, wdn_ref, wpg_ref, wpp_ref, g_ref, b_ref, y_ref, *, d_ff, alpha):
    for r0 in range(0, x_ref.shape[0], FFN_SUB_ROWS):
        rows = pl.ds(r0, FFN_SUB_ROWS)
        x = x_ref[rows, :]
        xb = x.astype(BF16)
        gt = _dot(xb, wup_ref[:, :d_ff])
        up = _dot(xb, wup_ref[:, d_ff:])
        hid = (gt * _sigmoid(gt) * up).astype(BF16)
        ffn = _dot(hid, wdn_ref[...])
        ple = _sigmoid(_dot(xb, wpg_ref[...])) * _dot(p_ref[rows, :].astype(BF16), wpp_ref[...])
        y_ref[rows, :] = _layer_norm(alpha * x + ffn + ple, g_ref[...], b_ref[...])


def _const_spec(shape):
    zeros = (0,) * len(shape)
    return pl.BlockSpec(shape, lambda *_: zeros, pipeline_mode=pl.Buffered(1))


def _mixer(x, s0, pool0, consts, *, G, TT, C, offset, layer, apply_ln_in, alpha):
    nseq, T, _ = x.shape
    assert nseq % G == 0 and T % TT == 0 and TT % C == 0 and C % 16 == 0
    grid = (nseq // G, T // TT)
    kern = functools.partial(_mixer_kernel, G=G, TT=TT, C=C, offset=offset, layer=layer,
                             apply_ln_in=apply_ln_in, alpha=alpha)
    in_specs = [
        pl.BlockSpec((G, TT, D_MODEL), lambda i, t: (i, t, 0)),
        pl.BlockSpec((G, HG_HEADS, HG_DK, HG_DV), lambda i, t: (i, 0, 0, 0)),
        pl.BlockSpec((G, POOL_CARRY, POOL_WIDTH), lambda i, t: (i, 0, 0)),
    ] + [_const_spec(c.shape) for c in consts]
    out_specs = [
        pl.BlockSpec((G, TT, D_MODEL), lambda i, t: (i, t, 0)),
        pl.BlockSpec((G, HG_HEADS, HG_DK, HG_DV), lambda i, t: (i, 0, 0, 0)),
        pl.BlockSpec((G, POOL_CARRY, POOL_WIDTH), lambda i, t: (i, 0, 0)),
    ]
    out_shape = [
        jax.ShapeDtypeStruct((nseq, T, D_MODEL), F32),
        jax.ShapeDtypeStruct((nseq, HG_HEADS, HG_DK, HG_DV), F32),
        jax.ShapeDtypeStruct((nseq, POOL_CARRY, POOL_WIDTH), F32),
    ]
    rows = G * C
    slots = min(2, TT // C)
    scratch = [
        pltpu.VMEM((slots, rows, HG_WIDTH), F32),
        pltpu.VMEM((slots, rows, HG_WIDTH), BF16),
        pltpu.VMEM((slots, rows, HG_WIDTH), BF16),
        pltpu.VMEM((slots, rows, HG_WIDTH), BF16),
        pltpu.VMEM((slots, rows, HG_WIDTH), BF16),
        pltpu.VMEM((slots, rows, HG_WIDTH), F32),
        pltpu.VMEM((slots, rows, POOL_WIDTH), BF16),
        pltpu.VMEM((G, POOL_CARRY, POOL_WIDTH), F32),
        pltpu.VMEM((G, HG_HEADS, HG_DV, HG_DK), F32),
    ]
    return pl.pallas_call(
        kern,
        grid=grid,
        in_specs=in_specs,
        out_specs=out_specs,
        out_shape=out_shape,
        scratch_shapes=scratch,
        compiler_params=pltpu.CompilerParams(
            dimension_semantics=("parallel", "arbitrary"), vmem_limit_bytes=VMEM_LIMIT_BYTES),
        name="mixer",
    )(x, s0, pool0, *consts)


def _ffn(x, p, consts, *, TM, alpha):
    n, _ = x.shape
    assert n % TM == 0 and TM % FFN_SUB_ROWS == 0
    d_ff = consts[1].shape[0]
    kern = functools.partial(_ffn_kernel, d_ff=d_ff, alpha=alpha)
    return pl.pallas_call(
        kern,
        grid=(n // TM,),
        in_specs=[pl.BlockSpec((TM, D_MODEL), lambda i: (i, 0)),
                  pl.BlockSpec((TM, p.shape[1]), lambda i: (i, 0))] + [_const_spec(c.shape) for c in consts],
        out_specs=pl.BlockSpec((TM, D_MODEL), lambda i: (i, 0)),
        out_shape=jax.ShapeDtypeStruct((n, D_MODEL), F32),
        compiler_params=pltpu.CompilerParams(
            dimension_semantics=("parallel",), vmem_limit_bytes=VMEM_LIMIT_BYTES),
        name="ffn",
    )(x, p, *consts)


def kernel(x_prompt, x_sample, p_prompt, p_sample, state_hgrn, state_pool, ln_in_g, ln_in_b, lb_logits, w_in,
           hgrn_norm_g, w_branch_a, w_pool_mix, pool_scale, w_branch_b, w_out, ln1_g, ln1_b, w_ffn_up,
           w_ffn_down, w_ple_proj, w_ple_gate, ln2_g, ln2_b):
    depth = w_in.shape[0]
    alpha = float((2 * depth) ** 0.25)
    bp, tp, _ = x_prompt.shape
    bs, ts, _ = x_sample.shape
    assert w_in.shape[1:] == (D_MODEL, IN_COLS) and x_prompt.shape[2] == x_sample.shape[2] == D_MODEL
    assert state_hgrn.shape == (depth, bs, HG_HEADS, HG_DK, HG_DV)
    assert state_pool.shape == (depth, bs, POOL_BUF, POOL_WIDTH) and lb_logits.shape == (depth + 1, HG_WIDTH)
    cp, cs = min(tp, CHUNK), min(ts, CHUNK)
    gp, gs = _seqs_per_block(bp, cp), _seqs_per_block(bs, cs)
    ttp, tts = cp * min(MIXER_CHUNKS, tp // cp), cs * min(MIXER_CHUNKS, ts // cs)
    row = lambda a: a.reshape(1, -1).astype(F32)

    xp, xs = x_prompt, x_sample
    hp, pp, hs_, ps = [], [], [], []
    for i in range(depth):
        mixer_consts = (row(ln_in_g), row(ln_in_b), lb_logits.astype(F32), w_in[i].astype(BF16),
                        row(hgrn_norm_g[i]), w_branch_a[i].astype(BF16), w_pool_mix[i].astype(BF16),
                        row(pool_scale[i]), w_branch_b[i].astype(BF16), w_out[i].astype(BF16),
                        row(ln1_g[i]), row(ln1_b[i]))
        ffn_consts = (w_ffn_up[i].astype(BF16), w_ffn_down[i].astype(BF16), w_ple_gate[i].astype(BF16),
                      w_ple_proj[i].astype(BF16), row(ln2_g[i]), row(ln2_b[i]))
        s0p = jnp.zeros((bp, HG_HEADS, HG_DK, HG_DV), F32)
        pool0p = jnp.zeros((bp, POOL_CARRY, POOL_WIDTH), F32)
        pool0s = jnp.pad(state_pool[i].astype(F32), ((0, 0), (POOL_CARRY - POOL_BUF, 0), (0, 0)))

        xp, sp, poolp = _mixer(xp, s0p, pool0p, mixer_consts, G=gp, TT=ttp, C=cp, offset=0, layer=i,
                               apply_ln_in=(i == 0), alpha=alpha)
        xs, ss, pools = _mixer(xs, state_hgrn[i].astype(F32), pool0s, mixer_consts, G=gs, TT=tts, C=cs,
                               offset=PAST_LEN, layer=i, apply_ln_in=(i == 0), alpha=alpha)
        xp = _ffn(xp.reshape(bp * tp, D_MODEL), p_prompt[i].reshape(bp * tp, -1), ffn_consts,
                  TM=min(FFN_ROWS, bp * tp), alpha=alpha).reshape(bp, tp, D_MODEL)
        xs = _ffn(xs.reshape(bs * ts, D_MODEL), p_sample[i].reshape(bs * ts, -1), ffn_consts,
                  TM=min(FFN_ROWS, bs * ts), alpha=alpha).reshape(bs, ts, D_MODEL)
        hp.append(sp)
        pp.append(poolp[:, POOL_CARRY - POOL_BUF:])
        hs_.append(ss)
        ps.append(pools[:, POOL_CARRY - POOL_BUF:])
    return (xp, xs, jnp.stack(hp, axis=0), jnp.stack(pp, axis=0), jnp.stack(hs_, axis=0), jnp.stack(ps, axis=0))
```

```python
import functools

import jax
import jax.numpy as jnp
from jax import lax
from jax.experimental import pallas as pl
from jax.experimental.pallas import tpu as pltpu

F32 = jnp.float32
BF16 = jnp.bfloat16

D_MODEL = 1024
HG_HEADS = 8
HG_DK = 128
HG_DV = 128
HG_WIDTH = HG_HEADS * HG_DV
POOL_WINDOWS = (2, 4, 8, 16)
POOL_GC = 128
POOL_WIDTH = len(POOL_WINDOWS) * POOL_GC
POOL_BUF = 15
POOL_CARRY = 16
LN_EPS = 1e-5
RMS_EPS = 1e-6
EXP_CLAMP = 80.0
VMEM_LIMIT_BYTES = 56 * 1024 * 1024
CHUNK = 64
PAST_LEN = 1024
MIXER_ROWS = 256
MIXER_CHUNKS = 2
FFN_ROWS = 1024
FFN_SUB_ROWS = 256


def _seqs_per_block(nseq, chunk):
    g = max(1, min(nseq, MIXER_ROWS // chunk))
    while nseq % g:
        g -= 1
    return g


_Q0, _F0, _I0, _G0 = 0, HG_WIDTH, 2 * HG_WIDTH, 3 * HG_WIDTH
_V0 = 4 * HG_WIDTH
_GA0 = _V0 + POOL_WIDTH
_GB0 = _GA0 + D_MODEL
IN_COLS = _GB0 + D_MODEL


def _sigmoid(x):
    return 0.5 * jnp.tanh(0.5 * x) + 0.5


def _layer_norm(x, g, b):
    mu = jnp.mean(x, axis=-1, keepdims=True)
    xc = x - mu
    var = jnp.mean(xc * xc, axis=-1, keepdims=True)
    return xc * lax.rsqrt(var + LN_EPS) * g + b


def _dot(a, b):
    return jnp.dot(a, b, preferred_element_type=F32)


def _mixer_kernel(x_ref, s0_ref, pool0_ref, lng_ref, lnb_ref, lbl_ref, win_ref, hgg_ref, wa_ref,
                  wpm_ref, psc_ref, wb_ref, wo_ref, l1g_ref, l1b_ref,
                  y_ref, sout_ref, poolout_ref,
                  b_s, qd_s, kdm_s, kde_s, iv_s, o_s, pm_s, hist_s, st_s,
                  *, G, TT, C, offset, layer, apply_ln_in, alpha):
    tb = pl.program_id(1)
    ntb = pl.num_programs(1)
    R = G * C
    n_chunks = TT // C
    pairs = [(g, h) for g in range(G) for h in range(HG_HEADS)]
    hs_of = lambda h: pl.ds(h * HG_DK, HG_DK)
    rows_of = lambda g: pl.ds(g * C, C)
    nt_dims = (((1,), (1,)), ((), ()))
    tn_dims = (((0,), (0,)), ((), ()))

    @pl.when(tb == 0)
    def _():
        for g, h in pairs:
            st_s[g, h] = s0_ref[g, h].T
        for g in range(G):
            hist_s[g] = pool0_ref[g]

    lbl = lbl_ref[...]
    lbe = jnp.exp(lbl - jnp.max(lbl, axis=0, keepdims=True))
    lb = jnp.sum(lbe[0:layer + 1], axis=0, keepdims=True) / jnp.sum(lbe, axis=0, keepdims=True)

    rr = lax.broadcasted_iota(jnp.int32, (R, R), 0)
    cc = lax.broadcasted_iota(jnp.int32, (R, R), 1)
    shift = C.bit_length() - 1
    tri_blk = jnp.where(((rr >> shift) == (cc >> shift)) & (cc <= rr), 1.0, 0.0).astype(BF16)
    ti = lax.broadcasted_iota(jnp.int32, (C, C), 0)
    si = lax.broadcasted_iota(jnp.int32, (C, C), 1)
    causal = si <= ti

    for ch in range(n_chunks):
        slot = ch % b_s.shape[0]
        b_c, qd_c, kdm_c, kde_c, iv_c = b_s.at[slot], qd_s.at[slot], kdm_s.at[slot], kde_s.at[slot], iv_s.at[slot]
        o_c, pm_c = o_s.at[slot], pm_s.at[slot]
        trows = pl.ds(ch * C, C)

        x = x_ref[:, trows, :].reshape(R, D_MODEL)
        xn = _layer_norm(x, lng_ref[...], lnb_ref[...]) if apply_ln_in else x
        xb = xn.astype(BF16)

        def proj(lo, hi):
            return _dot(xb, win_ref[:, lo:hi])

        q = proj(_Q0, _F0)
        f = lb + (1.0 - lb) * _sigmoid(proj(_F0, _I0))
        logf = jnp.log(f)
        k = 1.0 - f
        iv_c[...] = proj(_I0, _G0).astype(BF16)

        logf_hi = logf.astype(BF16)
        logf_lo = (logf - logf_hi.astype(F32)).astype(BF16)
        b = _dot(tri_blk, logf_hi) + _dot(tri_blk, logf_lo)
        b_c[...] = b

        bc = jnp.concatenate(
            [jnp.broadcast_to(b_c[pl.ds(g * C + C - 1, 1), :], (C, HG_WIDTH)) for g in range(G)], axis=0)
        bm = 0.5 * bc
        qd_c[...] = (q * jnp.exp(jnp.minimum(b - bm, EXP_CLAMP))).astype(BF16)
        kdm_c[...] = (k * jnp.exp(jnp.minimum(bm - b, EXP_CLAMP))).astype(BF16)
        kde_c[...] = (k * jnp.exp(bc - b)).astype(BF16)

        res = {}
        for g, h in pairs:
            rows, hs = rows_of(g), hs_of(h)
            em = jnp.exp(0.5 * b_c[pl.ds(g * C + C - 1, 1), hs])
            w = jnp.concatenate([(st_s[g, h] * em).astype(BF16), kdm_c[rows, hs]], axis=0)
            res[g, h] = lax.dot_general(qd_c[rows, hs], w, nt_dims, preferred_element_type=F32)

        v = proj(_V0, _GA0)
        pos1 = (offset + 1 + tb * TT + ch * C + lax.broadcasted_iota(jnp.int32, (C, POOL_GC), 0)).astype(F32)
        wins = [[] for _ in POOL_WINDOWS]
        for g in range(G):
            ext = jnp.concatenate([hist_s[g], v[g * C:(g + 1) * C]], axis=0)
            tail = ext[C:C + POOL_CARRY]
            if ch == n_chunks - 1:
                poolout_ref[g] = tail
            hist_s[g] = tail
            acc = ext
            for j, w_len in enumerate(POOL_WINDOWS):
                acc = acc + pltpu.roll(acc, w_len // 2, 0)
                wins[j].append(acc[POOL_CARRY:, :POOL_GC])
                acc = acc[:, POOL_GC:]
        for j, w_len in enumerate(POOL_WINDOWS):
            js = pl.ds(j * POOL_GC, POOL_GC)
            win = jnp.concatenate(wins[j], axis=0) if G > 1 else wins[j][0]
            cnt = jnp.minimum(pos1, float(w_len))
            cnt = jnp.concatenate([cnt] * G, axis=0) if G > 1 else cnt
            pooled = win / cnt - v[:, j * POOL_GC:(j + 1) * POOL_GC]
            pm_c[:, js] = (_dot(pooled.astype(BF16), wpm_ref[j]) * psc_ref[:, js]).astype(BF16)
        yb = _sigmoid(proj(_GB0, IN_COLS)) * _dot(pm_c[...], wb_ref[...])

        for g, h in pairs:
            rows, hs = rows_of(g), hs_of(h)
            ec = jnp.exp(b_c[pl.ds(g * C + C - 1, 1), hs])
            st_s[g, h] = st_s[g, h] * ec + lax.dot_general(iv_c[rows, hs], kde_c[rows, hs], tn_dims,
                                                           preferred_element_type=F32)
        gate = proj(_G0, _V0)
        gate = gate * _sigmoid(gate)
        ga = _sigmoid(proj(_GA0, _GB0))
        for g, h in pairs:
            rows, hs = rows_of(g), hs_of(h)
            p = jnp.where(causal, res[g, h][:, HG_DV:], 0.0).astype(BF16)
            o = _dot(p, iv_c[rows, hs]) + res[g, h][:, :HG_DV]
            o_c[rows, hs] = o

        ones_blk = jnp.ones((HG_DV, HG_DV), BF16)
        for h in range(HG_HEADS):
            hs = hs_of(h)
            oh = o_c[:, hs]
            ss = _dot((oh * oh).astype(BF16), ones_blk)
            o_c[:, hs] = oh * lax.rsqrt(ss * (1.0 / HG_DV) + RMS_EPS) * hgg_ref[:, hs]

        ya = _dot((o_c[...] * gate).astype(BF16), wa_ref[...])
        m = ga * ya + yb
        z = alpha * xn + _dot(m.astype(BF16), wo_ref[...])
        y_ref[:, trows, :] = _layer_norm(z, l1g_ref[...], l1b_ref[...]).reshape(G, C, D_MODEL)

    @pl.when(tb == ntb - 1)
    def _():
        for g, h in pairs:
            sout_ref[g, h] = st_s[g, h].T


def _ffn_kernel(x_ref, p_ref, wup_ref, wdn_ref, wpg_ref, wpp_ref, g_ref, b_ref, y_ref, *, d_ff, alpha):
    for r0 in range(0, x_ref.shape[0], FFN_SUB_ROWS):
        rows = pl.ds(r0, FFN_SUB_ROWS)
        x = x_ref[rows, :]
        xb = x.astype(BF16)
        gt = _dot(xb, wup_ref[:, :d_ff])
        up = _dot(xb, wup_ref[:, d_ff:])
        hid = (gt * _sigmoid(gt) * up).astype(BF16)
        ffn = _dot(hid, wdn_ref[...])
        ple = _sigmoid(_dot(xb, wpg_ref[...])) * _dot(p_ref[rows, :].astype(BF16), wpp_ref[...])
        y_ref[rows, :] = _layer_norm(alpha * x + ffn + ple, g_ref[...], b_ref[...])


def _const_spec(shape):
    zeros = (0,) * len(shape)
    return pl.BlockSpec(shape, lambda *_: zeros, pipeline_mode=pl.Buffered(1))


def _mixer(x, s0, pool0, consts, *, G, TT, C, offset, layer, apply_ln_in, alpha):
    nseq, T, _ = x.shape
    assert nseq % G == 0 and T % TT == 0 and TT % C == 0 and C % 16 == 0
    grid = (nseq // G, T // TT)
    kern = functools.partial(_mixer_kernel, G=G, TT=TT, C=C, offset=offset, layer=layer,
                             apply_ln_in=apply_ln_in, alpha=alpha)
    in_specs = [
        pl.BlockSpec((G, TT, D_MODEL), lambda i, t: (i, t, 0)),
        pl.BlockSpec((G, HG_HEADS, HG_DK, HG_DV), lambda i, t: (i, 0, 0, 0)),
        pl.BlockSpec((G, POOL_CARRY, POOL_WIDTH), lambda i, t: (i, 0, 0)),
    ] + [_const_spec(c.shape) for c in consts]
    out_specs = [
        pl.BlockSpec((G, TT, D_MODEL), lambda i, t: (i, t, 0)),
        pl.BlockSpec((G, HG_HEADS, HG_DK, HG_DV), lambda i, t: (i, 0, 0, 0)),
        pl.BlockSpec((G, POOL_CARRY, POOL_WIDTH), lambda i, t: (i, 0, 0)),
    ]
    out_shape = [
        jax.ShapeDtypeStruct((nseq, T, D_MODEL), F32),
        jax.ShapeDtypeStruct((nseq, HG_HEADS, HG_DK, HG_DV), F32),
        jax.ShapeDtypeStruct((nseq, POOL_CARRY, POOL_WIDTH), F32),
    ]
    rows = G * C
    slots = min(2, TT // C)
    scratch = [
        pltpu.VMEM((slots, rows, HG_WIDTH), F32),
        pltpu.VMEM((slots, rows, HG_WIDTH), BF16),
        pltpu.VMEM((slots, rows, HG_WIDTH), BF16),
        pltpu.VMEM((slots, rows, HG_WIDTH), BF16),
        pltpu.VMEM((slots, rows, HG_WIDTH), BF16),
        pltpu.VMEM((slots, rows, HG_WIDTH), F32),
        pltpu.VMEM((slots, rows, POOL_WIDTH), BF16),
        pltpu.VMEM((G, POOL_CARRY, POOL_WIDTH), F32),
        pltpu.VMEM((G, HG_HEADS, HG_DV, HG_DK), F32),
    ]
    return pl.pallas_call(
        kern,
        grid=grid,
        in_specs=in_specs,
        out_specs=out_specs,
        out_shape=out_shape,
        scratch_shapes=scratch,
        compiler_params=pltpu.CompilerParams(
            dimension_semantics=("parallel", "arbitrary"), vmem_limit_bytes=VMEM_LIMIT_BYTES),
        name="mixer",
    )(x, s0, pool0, *consts)


def _ffn(x, p, consts, *, TM, alpha):
    n, _ = x.shape
    assert n % TM == 0 and TM % FFN_SUB_ROWS == 0
    d_ff = consts[1].shape[0]
    kern = functools.partial(_ffn_kernel, d_ff=d_ff, alpha=alpha)
    return pl.pallas_call(
        kern,
        grid=(n // TM,),
        in_specs=[pl.BlockSpec((TM, D_MODEL), lambda i: (i, 0)),
                  pl.BlockSpec((TM, p.shape[1]), lambda i: (i, 0))] + [_const_spec(c.shape) for c in consts],
        out_specs=pl.BlockSpec((TM, D_MODEL), lambda i: (i, 0)),
        out_shape=jax.ShapeDtypeStruct((n, D_MODEL), F32),
        compiler_params=pltpu.CompilerParams(
            dimension_semantics=("parallel",), vmem_limit_bytes=VMEM_LIMIT_BYTES),
        name="ffn",
    )(x, p, *consts)


def kernel(x_prompt, x_sample, p_prompt, p_sample, state_hgrn, state_pool, ln_in_g, ln_in_b, lb_logits, w_in,
           hgrn_norm_g, w_branch_a, w_pool_mix, pool_scale, w_branch_b, w_out, ln1_g, ln1_b, w_ffn_up,
           w_ffn_down, w_ple_proj, w_ple_gate, ln2_g, ln2_b):
    depth = w_in.shape[0]
    alpha = float((2 * depth) ** 0.25)
    bp, tp, _ = x_prompt.shape
    bs, ts, _ = x_sample.shape
    assert w_in.shape[1:] == (D_MODEL, IN_COLS) and x_prompt.shape[2] == x_sample.shape[2] == D_MODEL
    assert state_hgrn.shape == (depth, bs, HG_HEADS, HG_DK, HG_DV)
    assert state_pool.shape == (depth, bs, POOL_BUF, POOL_WIDTH) and lb_logits.shape == (depth + 1, HG_WIDTH)
    cp, cs = min(tp, CHUNK), min(ts, CHUNK)
    gp, gs = _seqs_per_block(bp, cp), _seqs_per_block(bs, cs)
    ttp, tts = cp * min(MIXER_CHUNKS, tp // cp), cs * min(MIXER_CHUNKS, ts // cs)
    row = lambda a: a.reshape(1, -1).astype(F32)

    xp, xs = x_prompt, x_sample
    hp, pp, hs_, ps = [], [], [], []
    for i in range(depth):
        mixer_consts = (row(ln_in_g), row(ln_in_b), lb_logits.astype(F32), w_in[i].astype(BF16),
                        row(hgrn_norm_g[i]), w_branch_a[i].astype(BF16), w_pool_mix[i].astype(BF16),
                        row(pool_scale[i]), w_branch_b[i].astype(BF16), w_out[i].astype(BF16),
                        row(ln1_g[i]), row(ln1_b[i]))
        ffn_consts = (w_ffn_up[i].astype(BF16), w_ffn_down[i].astype(BF16), w_ple_gate[i].astype(BF16),
                      w_ple_proj[i].astype(BF16), row(ln2_g[i]), row(ln2_b[i]))
        s0p = jnp.zeros((bp, HG_HEADS, HG_DK, HG_DV), F32)
        pool0p = jnp.zeros((bp, POOL_CARRY, POOL_WIDTH), F32)
        pool0s = jnp.pad(state_pool[i].astype(F32), ((0, 0), (POOL_CARRY - POOL_BUF, 0), (0, 0)))

        xp, sp, poolp = _mixer(xp, s0p, pool0p, mixer_consts, G=gp, TT=ttp, C=cp, offset=0, layer=i,
                               apply_ln_in=(i == 0), alpha=alpha)
        xs, ss, pools = _mixer(xs, state_hgrn[i].astype(F32), pool0s, mixer_consts, G=gs, TT=tts, C=cs,
                               offset=PAST_LEN, layer=i, apply_ln_in=(i == 0), alpha=alpha)
        xp = _ffn(xp.reshape(bp * tp, D_MODEL), p_prompt[i].reshape(bp * tp, -1), ffn_consts,
                  TM=min(FFN_ROWS, bp * tp), alpha=alpha).reshape(bp, tp, D_MODEL)
        xs = _ffn(xs.reshape(bs * ts, D_MODEL), p_sample[i].reshape(bs * ts, -1), ffn_consts,
                  TM=min(FFN_ROWS, bs * ts), alpha=alpha).reshape(bs, ts, D_MODEL)
        hp.append(sp)
        pp.append(poolp[:, POOL_CARRY - POOL_BUF:])
        hs_.append(ss)
        ps.append(pools[:, POOL_CARRY - POOL_BUF:])
    return (xp, xs, jnp.stack(hp, axis=0), jnp.stack(pp, axis=0), jnp.stack(hs_, axis=0), jnp.stack(ps, axis=0))
```

```python
import functools

import jax
import jax.numpy as jnp
from jax import lax
from jax.experimental import pallas as pl
from jax.experimental.pallas import tpu as pltpu

F32 = jnp.float32
BF16 = jnp.bfloat16

D_MODEL = 1024
HG_HEADS = 8
HG_DK = 128
HG_DV = 128
HG_WIDTH = HG_HEADS * HG_DV
POOL_WINDOWS = (2, 4, 8, 16)
POOL_GC = 128
POOL_WIDTH = len(POOL_WINDOWS) * POOL_GC
POOL_BUF = 15
POOL_CARRY = 16
LN_EPS = 1e-5
RMS_EPS = 1e-6
EXP_CLAMP = 80.0
VMEM_LIMIT_BYTES = 56 * 1024 * 1024
CHUNK = 64
PAST_LEN = 1024
MIXER_ROWS = 256
MIXER_CHUNKS = 2
FILL_COLS = 256
FFN_ROWS = 1024
FFN_SUB_ROWS = 256


def _seqs_per_block(nseq, chunk):
    g = max(1, min(nseq, MIXER_ROWS // chunk))
    while nseq % g:
        g -= 1
    return g


_Q0, _F0, _I0, _G0 = 0, HG_WIDTH, 2 * HG_WIDTH, 3 * HG_WIDTH
_V0 = 4 * HG_WIDTH
_GA0 = _V0 + POOL_WIDTH
_GB0 = _GA0 + D_MODEL
IN_COLS = _GB0 + D_MODEL


def _sigmoid(x):
    return 0.5 * jnp.tanh(0.5 * x) + 0.5


def _layer_norm(x, g, b):
    mu = jnp.mean(x, axis=-1, keepdims=True)
    xc = x - mu
    var = jnp.mean(xc * xc, axis=-1, keepdims=True)
    return xc * lax.rsqrt(var + LN_EPS) * g + b


def _dot(a, b):
    return jnp.dot(a, b, preferred_element_type=F32)


def _mixer_kernel(x_ref, s0_ref, pool0_ref, lng_ref, lnb_ref, lbl_ref, win_ref, hgg_ref, wa_ref,
                  wpm_ref, psc_ref, wb_ref, wo_ref, l1g_ref, l1b_ref,
                  y_ref, sout_ref, poolout_ref,
                  b_s, qd_s, kdm_s, kde_s, iv_s, o_s, pm_s, hist_s, st_s,
                  *, G, TT, C, offset, layer, apply_ln_in, alpha):
    tb = pl.program_id(1)
    ntb = pl.num_programs(1)
    R = G * TT
    n_chunks = TT // C
    pairs = [(g, h) for g in range(G) for h in range(HG_HEADS)]
    hs_of = lambda h: pl.ds(h * HG_DK, HG_DK)
    rows_of = lambda g, ch: pl.ds(g * TT + ch * C, C)
    end_row = lambda g, ch: pl.ds(g * TT + ch * C + C - 1, 1)
    nt_dims = (((1,), (1,)), ((), ()))
    tn_dims = (((0,), (0,)), ((), ()))

    @pl.when(tb == 0)
    def _():
        for g, h in pairs:
            st_s[g, h] = s0_ref[g, h].T
        for g in range(G):
            hist_s[g] = pool0_ref[g]

    x = x_ref[...].reshape(R, D_MODEL)
    xn = _layer_norm(x, lng_ref[...], lnb_ref[...]) if apply_ln_in else x
    xb = xn.astype(BF16)

    def proj(lo, hi):
        return _dot(xb, win_ref[:, lo:hi])

    lbl = lbl_ref[...]
    lbe = jnp.exp(lbl - jnp.max(lbl, axis=0, keepdims=True))
    lb = jnp.sum(lbe[0:layer + 1], axis=0, keepdims=True) / jnp.sum(lbe, axis=0, keepdims=True)

    q = proj(_Q0, _F0)
    f = lb + (1.0 - lb) * _sigmoid(proj(_F0, _I0))
    logf = jnp.log(f)
    k = 1.0 - f
    iv_s[...] = proj(_I0, _G0).astype(BF16)

    rr = lax.broadcasted_iota(jnp.int32, (TT, TT), 0)
    cc = lax.broadcasted_iota(jnp.int32, (TT, TT), 1)
    shift = C.bit_length() - 1
    tri_blk = jnp.where(((rr >> shift) == (cc >> shift)) & (cc <= rr), 1.0, 0.0).astype(BF16)
    logf_hi = logf.astype(BF16)
    logf_lo = (logf - logf_hi.astype(F32)).astype(BF16)
    b = jnp.concatenate(
        [_dot(tri_blk, logf_hi[g * TT:(g + 1) * TT]) + _dot(tri_blk, logf_lo[g * TT:(g + 1) * TT]) for g in range(G)],
        axis=0)
    b_s[...] = b

    bc = jnp.concatenate(
        [jnp.broadcast_to(b_s[end_row(g, ch), :], (C, HG_WIDTH)) for g in range(G) for ch in range(n_chunks)], axis=0)
    bm = 0.5 * bc
    qd_s[...] = (q * jnp.exp(jnp.minimum(b - bm, EXP_CLAMP))).astype(BF16)
    kdm_s[...] = (k * jnp.exp(jnp.minimum(bm - b, EXP_CLAMP))).astype(BF16)
    kde_s[...] = (k * jnp.exp(bc - b)).astype(BF16)

    dense = {}

    def proj_piece(name, lo, hi):
        def run():
            dense.setdefault(name, []).append(proj(lo, hi))
        return run

    def pool_piece():
        v = jnp.concatenate(dense["v"], axis=1)
        pos1 = (offset + 1 + tb * TT + lax.broadcasted_iota(jnp.int32, (TT, 1), 0)).astype(F32)
        wins = [[] for _ in POOL_WINDOWS]
        for g in range(G):
            ext = jnp.concatenate([hist_s[g], v[g * TT:(g + 1) * TT]], axis=0)
            tail = ext[TT:TT + POOL_CARRY]
            poolout_ref[g] = tail
            hist_s[g] = tail
            acc = ext
            for j, w_len in enumerate(POOL_WINDOWS):
                acc = acc + pltpu.roll(acc, w_len // 2, 0)
                wins[j].append(acc[POOL_CARRY:, :POOL_GC])
                acc = acc[:, POOL_GC:]
        for j, w_len in enumerate(POOL_WINDOWS):
            js = pl.ds(j * POOL_GC, POOL_GC)
            win = jnp.concatenate(wins[j], axis=0) if G > 1 else wins[j][0]
            cnt = jnp.minimum(pos1, float(w_len))
            cnt = jnp.concatenate([cnt] * G, axis=0) if G > 1 else cnt
            pooled = win / cnt - v[:, j * POOL_GC:(j + 1) * POOL_GC]
            pm_s[:, js] = (_dot(pooled.astype(BF16), wpm_ref[j]) * psc_ref[:, js]).astype(BF16)

    def wb_piece(lo, hi):
        def run():
            dense.setdefault("yb", []).append(_dot(pm_s[...], wb_ref[:, lo:hi]))
        return run

    pieces = [proj_piece("gate", _G0 + j, _G0 + j + FILL_COLS) for j in range(0, HG_WIDTH, FILL_COLS)]
    pieces += [proj_piece("ga", _GA0 + j, _GA0 + j + FILL_COLS) for j in range(0, D_MODEL, FILL_COLS)]
    pieces += [proj_piece("v", _V0 + j, _V0 + j + FILL_COLS) for j in range(0, POOL_WIDTH, FILL_COLS)]
    pieces += [pool_piece]
    pieces += [proj_piece("gb", _GB0 + j, _GB0 + j + FILL_COLS) for j in range(0, D_MODEL, FILL_COLS)]
    pieces += [wb_piece(j, j + FILL_COLS) for j in range(0, D_MODEL, FILL_COLS)]
    slots = 3 * len(pairs) * n_chunks
    cadence = max(1, slots // len(pieces))
    issued = [0]

    def after_pair():
        issued[0] += 1
        if issued[0] % cadence == 0 and pieces:
            pieces.pop(0)()

    ti = lax.broadcasted_iota(jnp.int32, (C, C), 0)
    si = lax.broadcasted_iota(jnp.int32, (C, C), 1)
    causal = si <= ti
    for ch in range(n_chunks):
        res = {}
        for g, h in pairs:
            rows, hs = rows_of(g, ch), hs_of(h)
            em = jnp.exp(0.5 * b_s[end_row(g, ch), hs])
            w = jnp.concatenate([(st_s[g, h] * em).astype(BF16), kdm_s[rows, hs]], axis=0)
            res[g, h] = lax.dot_general(qd_s[rows, hs], w, nt_dims, preferred_element_type=F32)
            after_pair()
        for g, h in pairs:
            rows, hs = rows_of(g, ch), hs_of(h)
            ec = jnp.exp(b_s[end_row(g, ch), hs])
            st_s[g, h] = st_s[g, h] * ec + lax.dot_general(iv_s[rows, hs], kde_s[rows, hs], tn_dims,
                                                           preferred_element_type=F32)
            after_pair()
        for g, h in pairs:
            rows, hs = rows_of(g, ch), hs_of(h)
            p = jnp.where(causal, res[g, h][:, HG_DV:], 0.0).astype(BF16)
            o_s[rows, hs] = _dot(p, iv_s[rows, hs]) + res[g, h][:, :HG_DV]
            after_pair()
    while pieces:
        pieces.pop(0)()

    ones_blk = jnp.ones((HG_DV, HG_DV), BF16)
    for h in range(HG_HEADS):
        hs = hs_of(h)
        oh = o_s[:, hs]
        ss = _dot((oh * oh).astype(BF16), ones_blk)
        o_s[:, hs] = oh * lax.rsqrt(ss * (1.0 / HG_DV) + RMS_EPS) * hgg_ref[:, hs]

    gate = jnp.concatenate(dense["gate"], axis=1)
    ya = _dot((o_s[...] * (gate * _sigmoid(gate))).astype(BF16), wa_ref[...])
    yb = _sigmoid(jnp.concatenate(dense["gb"], axis=1)) * jnp.concatenate(dense["yb"], axis=1)
    m = _sigmoid(jnp.concatenate(dense["ga"], axis=1)) * ya + yb
    z = alpha * xn + _dot(m.astype(BF16), wo_ref[...])
    y_ref[...] = _layer_norm(z, l1g_ref[...], l1b_ref[...]).reshape(G, TT, D_MODEL)

    @pl.when(tb == ntb - 1)
    def _():
        for g, h in pairs:
            sout_ref[g, h] = st_s[g, h].T


def _ffn_kernel(x_ref, p_ref, wup_ref, wdn_ref, wpg_ref, wpp_ref, g_ref, b_ref, y_ref, *, d_ff, alpha):
    for r0 in range(0, x_ref.shape[0], FFN_SUB_ROWS):
        rows = pl.ds(r0, FFN_SUB_ROWS)
        x = x_ref[rows, :]
        xb = x.astype(BF16)
        gt = _dot(xb, wup_ref[:, :d_ff])
        up = _dot(xb, wup_ref[:, d_ff:])
        hid = (gt * _sigmoid(gt) * up).astype(BF16)
        ffn = _dot(hid, wdn_ref[...])
        ple = _sigmoid(_dot(xb, wpg_ref[...])) * _dot(p_ref[rows, :].astype(BF16), wpp_ref[...])
        y_ref[rows, :] = _layer_norm(alpha * x + ffn + ple, g_ref[...], b_ref[...])


def _const_spec(shape):
    zeros = (0,) * len(shape)
    return pl.BlockSpec(shape, lambda *_: zeros, pipeline_mode=pl.Buffered(1))


def _mixer(x, s0, pool0, consts, *, G, TT, C, offset, layer, apply_ln_in, alpha):
    nseq, T, _ = x.shape
    assert nseq % G == 0 and T % TT == 0 and TT % C == 0 and C % 16 == 0
    grid = (nseq // G, T // TT)
    kern = functools.partial(_mixer_kernel, G=G, TT=TT, C=C, offset=offset, layer=layer,
                             apply_ln_in=apply_ln_in, alpha=alpha)
    in_specs = [
        pl.BlockSpec((G, TT, D_MODEL), lambda i, t: (i, t, 0)),
        pl.BlockSpec((G, HG_HEADS, HG_DK, HG_DV), lambda i, t: (i, 0, 0, 0)),
        pl.BlockSpec((G, POOL_CARRY, POOL_WIDTH), lambda i, t: (i, 0, 0)),
    ] + [_const_spec(c.shape) for c in consts]
    out_specs = [
        pl.BlockSpec((G, TT, D_MODEL), lambda i, t: (i, t, 0)),
        pl.BlockSpec((G, HG_HEADS, HG_DK, HG_DV), lambda i, t: (i, 0, 0, 0)),
        pl.BlockSpec((G, POOL_CARRY, POOL_WIDTH), lambda i, t: (i, 0, 0)),
    ]
    out_shape = [
        jax.ShapeDtypeStruct((nseq, T, D_MODEL), F32),
        jax.ShapeDtypeStruct((nseq, HG_HEADS, HG_DK, HG_DV), F32),
        jax.ShapeDtypeStruct((nseq, POOL_CARRY, POOL_WIDTH), F32),
    ]
    rows = G * TT
    scratch = [
        pltpu.VMEM((rows, HG_WIDTH), F32),
        pltpu.VMEM((rows, HG_WIDTH), BF16),
        pltpu.VMEM((rows, HG_WIDTH), BF16),
        pltpu.VMEM((rows, HG_WIDTH), BF16),
        pltpu.VMEM((rows, HG_WIDTH), BF16),
        pltpu.VMEM((rows, HG_WIDTH), F32),
        pltpu.VMEM((rows, POOL_WIDTH), BF16),
        pltpu.VMEM((G, POOL_CARRY, POOL_WIDTH), F32),
        pltpu.VMEM((G, HG_HEADS, HG_DV, HG_DK), F32),
    ]
    return pl.pallas_call(
        kern,
        grid=grid,
        in_specs=in_specs,
        out_specs=out_specs,
        out_shape=out_shape,
        scratch_shapes=scratch,
        compiler_params=pltpu.CompilerParams(
            dimension_semantics=("parallel", "arbitrary"), vmem_limit_bytes=VMEM_LIMIT_BYTES),
        name="mixer",
    )(x, s0, pool0, *consts)


def _ffn(x, p, consts, *, TM, alpha):
    n, _ = x.shape
    assert n % TM == 0 and TM % FFN_SUB_ROWS == 0
    d_ff = consts[1].shape[0]
    kern = functools.partial(_ffn_kernel, d_ff=d_ff, alpha=alpha)
    return pl.pallas_call(
        kern,
        grid=(n // TM,),
        in_specs=[pl.BlockSpec((TM, D_MODEL), lambda i: (i, 0)),
                  pl.BlockSpec((TM, p.shape[1]), lambda i: (i, 0))] + [_const_spec(c.shape) for c in consts],
        out_specs=pl.BlockSpec((TM, D_MODEL), lambda i: (i, 0)),
        out_shape=jax.ShapeDtypeStruct((n, D_MODEL), F32),
        compiler_params=pltpu.CompilerParams(
            dimension_semantics=("parallel",), vmem_limit_bytes=VMEM_LIMIT_BYTES),
        name="ffn",
    )(x, p, *consts)


def kernel(x_prompt, x_sample, p_prompt, p_sample, state_hgrn, state_pool, ln_in_g, ln_in_b, lb_logits, w_in,
           hgrn_norm_g, w_branch_a, w_pool_mix, pool_scale, w_branch_b, w_out, ln1_g, ln1_b, w_ffn_up,
           w_ffn_down, w_ple_proj, w_ple_gate, ln2_g, ln2_b):
    depth = w_in.shape[0]
    alpha = float((2 * depth) ** 0.25)
    bp, tp, _ = x_prompt.shape
    bs, ts, _ = x_sample.shape
    assert w_in.shape[1:] == (D_MODEL, IN_COLS) and x_prompt.shape[2] == x_sample.shape[2] == D_MODEL
    assert state_hgrn.shape == (depth, bs, HG_HEADS, HG_DK, HG_DV)
    assert state_pool.shape == (depth, bs, POOL_BUF, POOL_WIDTH) and lb_logits.shape == (depth + 1, HG_WIDTH)
    cp, cs = min(tp, CHUNK), min(ts, CHUNK)
    gp, gs = _seqs_per_block(bp, cp), _seqs_per_block(bs, cs)
    ttp, tts = cp * min(MIXER_CHUNKS, tp // cp), cs * min(MIXER_CHUNKS, ts // cs)
    row = lambda a: a.reshape(1, -1).astype(F32)

    xp, xs = x_prompt, x_sample
    hp, pp, hs_, ps = [], [], [], []
    for i in range(depth):
        mixer_consts = (row(ln_in_g), row(ln_in_b), lb_logits.astype(F32), w_in[i].astype(BF16),
                        row(hgrn_norm_g[i]), w_branch_a[i].astype(BF16), w_pool_mix[i].astype(BF16),
                        row(pool_scale[i]), w_branch_b[i].astype(BF16), w_out[i].astype(BF16),
                        row(ln1_g[i]), row(ln1_b[i]))
        ffn_consts = (w_ffn_up[i].astype(BF16), w_ffn_down[i].astype(BF16), w_ple_gate[i].astype(BF16),
                      w_ple_proj[i].astype(BF16), row(ln2_g[i]), row(ln2_b[i]))
        s0p = jnp.zeros((bp, HG_HEADS, HG_DK, HG_DV), F32)
        pool0p = jnp.zeros((bp, POOL_CARRY, POOL_WIDTH), F32)
        pool0s = jnp.pad(state_pool[i].astype(F32), ((0, 0), (POOL_CARRY - POOL_BUF, 0), (0, 0)))

        xp, sp, poolp = _mixer(xp, s0p, pool0p, mixer_consts, G=gp, TT=ttp, C=cp, offset=0, layer=i,
                               apply_ln_in=(i == 0), alpha=alpha)
        xs, ss, pools = _mixer(xs, state_hgrn[i].astype(F32), pool0s, mixer_consts, G=gs, TT=tts, C=cs,
                               offset=PAST_LEN, layer=i, apply_ln_in=(i == 0), alpha=alpha)
        xp = _ffn(xp.reshape(bp * tp, D_MODEL), p_prompt[i].reshape(bp * tp, -1), ffn_consts,
                  TM=min(FFN_ROWS, bp * tp), alpha=alpha).reshape(bp, tp, D_MODEL)
        xs = _ffn(xs.reshape(bs * ts, D_MODEL), p_sample[i].reshape(bs * ts, -1), ffn_consts,
                  TM=min(FFN_ROWS, bs * ts), alpha=alpha).reshape(bs, ts, D_MODEL)
        hp.append(sp)
        pp.append(poolp[:, POOL_CARRY - POOL_BUF:])
        hs_.append(ss)
        ps.append(pools[:, POOL_CARRY - POOL_BUF:])
    return (xp, xs, jnp.stack(hp, axis=0), jnp.stack(pp, axis=0), jnp.stack(hs_, axis=0), jnp.stack(ps, axis=0))
```

```python
import functools

import jax
import jax.numpy as jnp
from jax import lax
from jax.experimental import pallas as pl
from jax.experimental.pallas import tpu as pltpu

F32 = jnp.float32
BF16 = jnp.bfloat16

D_MODEL = 1024
HG_HEADS = 8
HG_DK = 128
HG_DV = 128
HG_WIDTH = HG_HEADS * HG_DV
POOL_WINDOWS = (2, 4, 8, 16)
POOL_GC = 128
POOL_WIDTH = len(POOL_WINDOWS) * POOL_GC
POOL_BUF = 15
POOL_CARRY = 16
LN_EPS = 1e-5
RMS_EPS = 1e-6
EXP_CLAMP = 80.0
VMEM_LIMIT_BYTES = 56 * 1024 * 1024
CHUNK = 64
PAST_LEN = 1024
MIXER_ROWS = 256
MIXER_CHUNKS = 2
FILL_COLS = 256
STAGE_UNITS = (4, 1, 1)
FFN_ROWS = 1024
FFN_SUB_ROWS = 256


def _seqs_per_block(nseq, chunk):
    g = max(1, min(nseq, MIXER_ROWS // chunk))
    while nseq % g:
        g -= 1
    return g


_Q0, _F0, _I0, _G0 = 0, HG_WIDTH, 2 * HG_WIDTH, 3 * HG_WIDTH
_V0 = 4 * HG_WIDTH
_GA0 = _V0 + POOL_WIDTH
_GB0 = _GA0 + D_MODEL
IN_COLS = _GB0 + D_MODEL


def _sigmoid(x):
    return 0.5 * jnp.tanh(0.5 * x) + 0.5


def _layer_norm(x, g, b):
    mu = jnp.mean(x, axis=-1, keepdims=True)
    xc = x - mu
    var = jnp.mean(xc * xc, axis=-1, keepdims=True)
    return xc * lax.rsqrt(var + LN_EPS) * g + b


def _dot(a, b):
    return jnp.dot(a, b, preferred_element_type=F32)


def _mixer_kernel(x_ref, s0_ref, pool0_ref, lng_ref, lnb_ref, lbl_ref, win_ref, hgg_ref, wa_ref,
                  wpm_ref, psc_ref, wb_ref, wo_ref, l1g_ref, l1b_ref,
                  y_ref, sout_ref, poolout_ref,
                  b_s, qd_s, kdm_s, kde_s, iv_s, o_s, pm_s, hist_s, st_s,
                  *, G, TT, C, offset, layer, apply_ln_in, alpha):
    tb = pl.program_id(1)
    ntb = pl.num_programs(1)
    R = G * TT
    n_chunks = TT // C
    pairs = [(g, h) for g in range(G) for h in range(HG_HEADS)]
    hs_of = lambda h: pl.ds(h * HG_DK, HG_DK)
    rows_of = lambda g, ch: pl.ds(g * TT + ch * C, C)
    end_row = lambda g, ch: pl.ds(g * TT + ch * C + C - 1, 1)
    nt_dims = (((1,), (1,)), ((), ()))
    tn_dims = (((0,), (0,)), ((), ()))

    @pl.when(tb == 0)
    def _():
        for g, h in pairs:
            st_s[g, h] = s0_ref[g, h].T
        for g in range(G):
            hist_s[g] = pool0_ref[g]

    x = x_ref[...].reshape(R, D_MODEL)
    xn = _layer_norm(x, lng_ref[...], lnb_ref[...]) if apply_ln_in else x
    xb = xn.astype(BF16)

    def proj(lo, hi):
        return _dot(xb, win_ref[:, lo:hi])

    lbl = lbl_ref[...]
    lbe = jnp.exp(lbl - jnp.max(lbl, axis=0, keepdims=True))
    lb = jnp.sum(lbe[0:layer + 1], axis=0, keepdims=True) / jnp.sum(lbe, axis=0, keepdims=True)

    q = proj(_Q0, _F0)
    f = lb + (1.0 - lb) * _sigmoid(proj(_F0, _I0))
    logf = jnp.log(f)
    k = 1.0 - f

    rr = lax.broadcasted_iota(jnp.int32, (TT, TT), 0)
    cc = lax.broadcasted_iota(jnp.int32, (TT, TT), 1)
    shift = C.bit_length() - 1
    tri_blk = jnp.where(((rr >> shift) == (cc >> shift)) & (cc <= rr), 1.0, 0.0).astype(BF16)
    logf_hi = logf.astype(BF16)
    logf_lo = (logf - logf_hi.astype(F32)).astype(BF16)
    b = jnp.concatenate(
        [_dot(tri_blk, logf_hi[g * TT:(g + 1) * TT]) + _dot(tri_blk, logf_lo[g * TT:(g + 1) * TT]) for g in range(G)],
        axis=0)
    b_s[...] = b

    bc = jnp.concatenate(
        [jnp.broadcast_to(b_s[end_row(g, ch), :], (C, HG_WIDTH)) for g in range(G) for ch in range(n_chunks)], axis=0)
    bm = 0.5 * bc
    qd_s[...] = (q * jnp.exp(jnp.minimum(b - bm, EXP_CLAMP))).astype(BF16)
    kdm_s[...] = (k * jnp.exp(jnp.minimum(bm - b, EXP_CLAMP))).astype(BF16)
    kde_s[...] = (k * jnp.exp(bc - b)).astype(BF16)

    dense = {}

    def proj_piece(name, lo, hi):
        def run():
            dense.setdefault(name, []).append(proj(lo, hi))
        return run

    def pool_piece():
        v = jnp.concatenate(dense["v"], axis=1)
        pos1 = (offset + 1 + tb * TT + lax.broadcasted_iota(jnp.int32, (TT, 1), 0)).astype(F32)
        wins = [[] for _ in POOL_WINDOWS]
        for g in range(G):
            ext = jnp.concatenate([hist_s[g], v[g * TT:(g + 1) * TT]], axis=0)
            tail = ext[TT:TT + POOL_CARRY]
            poolout_ref[g] = tail
            hist_s[g] = tail
            acc = ext
            for j, w_len in enumerate(POOL_WINDOWS):
                acc = acc + pltpu.roll(acc, w_len // 2, 0)
                wins[j].append(acc[POOL_CARRY:, :POOL_GC])
                acc = acc[:, POOL_GC:]
        for j, w_len in enumerate(POOL_WINDOWS):
            js = pl.ds(j * POOL_GC, POOL_GC)
            win = jnp.concatenate(wins[j], axis=0) if G > 1 else wins[j][0]
            cnt = jnp.minimum(pos1, float(w_len))
            cnt = jnp.concatenate([cnt] * G, axis=0) if G > 1 else cnt
            pooled = win / cnt - v[:, j * POOL_GC:(j + 1) * POOL_GC]
            pm_s[:, js] = (_dot(pooled.astype(BF16), wpm_ref[j]) * psc_ref[:, js]).astype(BF16)

    def wb_piece(lo, hi):
        def run():
            dense.setdefault("yb", []).append(_dot(pm_s[...], wb_ref[:, lo:hi]))
        return run

    def values_piece(j):
        def run():
            iv_s[:, pl.ds(j, FILL_COLS)] = proj(_I0 + j, _I0 + j + FILL_COLS).astype(BF16)
        return run

    pieces = [values_piece(j) for j in range(0, HG_WIDTH, FILL_COLS)]
    n_value_pieces = len(pieces)
    pieces += [proj_piece("gate", _G0 + j, _G0 + j + FILL_COLS) for j in range(0, HG_WIDTH, FILL_COLS)]
    pieces += [proj_piece("ga", _GA0 + j, _GA0 + j + FILL_COLS) for j in range(0, D_MODEL, FILL_COLS)]
    pieces += [proj_piece("v", _V0 + j, _V0 + j + FILL_COLS) for j in range(0, POOL_WIDTH, FILL_COLS)]
    pieces += [pool_piece]
    pieces += [proj_piece("gb", _GB0 + j, _GB0 + j + FILL_COLS) for j in range(0, D_MODEL, FILL_COLS)]
    pieces += [wb_piece(j, j + FILL_COLS) for j in range(0, D_MODEL, FILL_COLS)]
    cadence = -(-sum(STAGE_UNITS) * len(pairs) * n_chunks // len(pieces))
    issued = [0, 0]

    def after_pair(units):
        issued[0] += units
        while pieces and issued[0] >= (issued[1] + 1) * cadence:
            issued[1] += 1
            pieces.pop(0)()

    ti = lax.broadcasted_iota(jnp.int32, (C, C), 0)
    si = lax.broadcasted_iota(jnp.int32, (C, C), 1)
    causal = si <= ti
    for ch in range(n_chunks):
        res = {}
        for g, h in pairs:
            rows, hs = rows_of(g, ch), hs_of(h)
            em = jnp.exp(0.5 * b_s[end_row(g, ch), hs])
            w = jnp.concatenate([(st_s[g, h] * em).astype(BF16), kdm_s[rows, hs]], axis=0)
            res[g, h] = lax.dot_general(qd_s[rows, hs], w, nt_dims, preferred_element_type=F32)
            after_pair(STAGE_UNITS[0])
        while issued[1] < n_value_pieces:
            issued[1] += 1
            pieces.pop(0)()
        for g, h in pairs:
            rows, hs = rows_of(g, ch), hs_of(h)
            ec = jnp.exp(b_s[end_row(g, ch), hs])
            st_s[g, h] = st_s[g, h] * ec + lax.dot_general(iv_s[rows, hs], kde_s[rows, hs], tn_dims,
                                                           preferred_element_type=F32)
            after_pair(STAGE_UNITS[1])
        for g, h in pairs:
            rows, hs = rows_of(g, ch), hs_of(h)
            p = jnp.where(causal, res[g, h][:, HG_DV:], 0.0).astype(BF16)
            o_s[rows, hs] = _dot(p, iv_s[rows, hs]) + res[g, h][:, :HG_DV]
            after_pair(STAGE_UNITS[2])
    while pieces:
        pieces.pop(0)()

    ones_blk = jnp.ones((HG_DV, HG_DV), BF16)
    for h in range(HG_HEADS):
        hs = hs_of(h)
        oh = o_s[:, hs]
        ss = _dot((oh * oh).astype(BF16), ones_blk)
        o_s[:, hs] = oh * lax.rsqrt(ss * (1.0 / HG_DV) + RMS_EPS) * hgg_ref[:, hs]

    gate = jnp.concatenate(dense["gate"], axis=1)
    ya = _dot((o_s[...] * (gate * _sigmoid(gate))).astype(BF16), wa_ref[...])
    yb = _sigmoid(jnp.concatenate(dense["gb"], axis=1)) * jnp.concatenate(dense["yb"], axis=1)
    m = _sigmoid(jnp.concatenate(dense["ga"], axis=1)) * ya + yb
    z = alpha * xn + _dot(m.astype(BF16), wo_ref[...])
    y_ref[...] = _layer_norm(z, l1g_ref[...], l1b_ref[...]).reshape(G, TT, D_MODEL)

    @pl.when(tb == ntb - 1)
    def _():
        for g, h in pairs:
            sout_ref[g, h] = st_s[g, h].T


def _ffn_kernel(x_ref, p_ref, wup_ref, wdn_ref, wpg_ref, wpp_ref, g_ref, b_ref, y_ref, *, d_ff, alpha):
    for r0 in range(0, x_ref.shape[0], FFN_SUB_ROWS):
        rows = pl.ds(r0, FFN_SUB_ROWS)
        x = x_ref[rows, :]
        xb = x.astype(BF16)
        gt = _dot(xb, wup_ref[:, :d_ff])
        up = _dot(xb, wup_ref[:, d_ff:])
        hid = (gt * _sigmoid(gt) * up).astype(BF16)
        ffn = _dot(hid, wdn_ref[...])
        ple = _sigmoid(_dot(xb, wpg_ref[...])) * _dot(p_ref[rows, :].astype(BF16), wpp_ref[...])
        y_ref[rows, :] = _layer_norm(alpha * x + ffn + ple, g_ref[...], b_ref[...])


def _const_spec(shape):
    zeros = (0,) * len(shape)
    return pl.BlockSpec(shape, lambda *_: zeros, pipeline_mode=pl.Buffered(1))


def _mixer(x, s0, pool0, consts, *, G, TT, C, offset, layer, apply_ln_in, alpha):
    nseq, T, _ = x.shape
    assert nseq % G == 0 and T % TT == 0 and TT % C == 0 and C % 16 == 0
    grid = (nseq // G, T // TT)
    kern = functools.partial(_mixer_kernel, G=G, TT=TT, C=C, offset=offset, layer=layer,
                             apply_ln_in=apply_ln_in, alpha=alpha)
    in_specs = [
        pl.BlockSpec((G, TT, D_MODEL), lambda i, t: (i, t, 0)),
        pl.BlockSpec((G, HG_HEADS, HG_DK, HG_DV), lambda i, t: (i, 0, 0, 0)),
        pl.BlockSpec((G, POOL_CARRY, POOL_WIDTH), lambda i, t: (i, 0, 0)),
    ] + [_const_spec(c.shape) for c in consts]
    out_specs = [
        pl.BlockSpec((G, TT, D_MODEL), lambda i, t: (i, t, 0)),
        pl.BlockSpec((G, HG_HEADS, HG_DK, HG_DV), lambda i, t: (i, 0, 0, 0)),
        pl.BlockSpec((G, POOL_CARRY, POOL_WIDTH), lambda i, t: (i, 0, 0)),
    ]
    out_shape = [
        jax.ShapeDtypeStruct((nseq, T, D_MODEL), F32),
        jax.ShapeDtypeStruct((nseq, HG_HEADS, HG_DK, HG_DV), F32),
        jax.ShapeDtypeStruct((nseq, POOL_CARRY, POOL_WIDTH), F32),
    ]
    rows = G * TT
    scratch = [
        pltpu.VMEM((rows, HG_WIDTH), F32),
        pltpu.VMEM((rows, HG_WIDTH), BF16),
        pltpu.VMEM((rows, HG_WIDTH), BF16),
        pltpu.VMEM((rows, HG_WIDTH), BF16),
        pltpu.VMEM((rows, HG_WIDTH), BF16),
        pltpu.VMEM((rows, HG_WIDTH), F32),
        pltpu.VMEM((rows, POOL_WIDTH), BF16),
        pltpu.VMEM((G, POOL_CARRY, POOL_WIDTH), F32),
        pltpu.VMEM((G, HG_HEADS, HG_DV, HG_DK), F32),
    ]
    return pl.pallas_call(
        kern,
        grid=grid,
        in_specs=in_specs,
        out_specs=out_specs,
        out_shape=out_shape,
        scratch_shapes=scratch,
        compiler_params=pltpu.CompilerParams(
            dimension_semantics=("parallel", "arbitrary"), vmem_limit_bytes=VMEM_LIMIT_BYTES),
        name="mixer",
    )(x, s0, pool0, *consts)


def _ffn(x, p, consts, *, TM, alpha):
    n, _ = x.shape
    assert n % TM == 0 and TM % FFN_SUB_ROWS == 0
    d_ff = consts[1].shape[0]
    kern = functools.partial(_ffn_kernel, d_ff=d_ff, alpha=alpha)
    return pl.pallas_call(
        kern,
        grid=(n // TM,),
        in_specs=[pl.BlockSpec((TM, D_MODEL), lambda i: (i, 0)),
                  pl.BlockSpec((TM, p.shape[1]), lambda i: (i, 0))] + [_const_spec(c.shape) for c in consts],
        out_specs=pl.BlockSpec((TM, D_MODEL), lambda i: (i, 0)),
        out_shape=jax.ShapeDtypeStruct((n, D_MODEL), F32),
        compiler_params=pltpu.CompilerParams(
            dimension_semantics=("parallel",), vmem_limit_bytes=VMEM_LIMIT_BYTES),
        name="ffn",
    )(x, p, *consts)


def kernel(x_prompt, x_sample, p_prompt, p_sample, state_hgrn, state_pool, ln_in_g, ln_in_b, lb_logits, w_in,
           hgrn_norm_g, w_branch_a, w_pool_mix, pool_scale, w_branch_b, w_out, ln1_g, ln1_b, w_ffn_up,
           w_ffn_down, w_ple_proj, w_ple_gate, ln2_g, ln2_b):
    depth = w_in.shape[0]
    alpha = float((2 * depth) ** 0.25)
    bp, tp, _ = x_prompt.shape
    bs, ts, _ = x_sample.shape
    assert w_in.shape[1:] == (D_MODEL, IN_COLS) and x_prompt.shape[2] == x_sample.shape[2] == D_MODEL
    assert state_hgrn.shape == (depth, bs, HG_HEADS, HG_DK, HG_DV)
    assert state_pool.shape == (depth, bs, POOL_BUF, POOL_WIDTH) and lb_logits.shape == (depth + 1, HG_WIDTH)
    cp, cs = min(tp, CHUNK), min(ts, CHUNK)
    gp, gs = _seqs_per_block(bp, cp), _seqs_per_block(bs, cs)
    ttp, tts = cp * min(MIXER_CHUNKS, tp // cp), cs * min(MIXER_CHUNKS, ts // cs)
    row = lambda a: a.reshape(1, -1).astype(F32)

    xp, xs = x_prompt, x_sample
    hp, pp, hs_, ps = [], [], [], []
    for i in range(depth):
        mixer_consts = (row(ln_in_g), row(ln_in_b), lb_logits.astype(F32), w_in[i].astype(BF16),
                        row(hgrn_norm_g[i]), w_branch_a[i].astype(BF16), w_pool_mix[i].astype(BF16),
                        row(pool_scale[i]), w_branch_b[i].astype(BF16), w_out[i].astype(BF16),
                        row(ln1_g[i]), row(ln1_b[i]))
        ffn_consts = (w_ffn_up[i].astype(BF16), w_ffn_down[i].astype(BF16), w_ple_gate[i].astype(BF16),
                      w_ple_proj[i].astype(BF16), row(ln2_g[i]), row(ln2_b[i]))
        s0p = jnp.zeros((bp, HG_HEADS, HG_DK, HG_DV), F32)
        pool0p = jnp.zeros((bp, POOL_CARRY, POOL_WIDTH), F32)
        pool0s = jnp.pad(state_pool[i].astype(F32), ((0, 0), (POOL_CARRY - POOL_BUF, 0), (0, 0)))

        xp, sp, poolp = _mixer(xp, s0p, pool0p, mixer_consts, G=gp, TT=ttp, C=cp, offset=0, layer=i,
                               apply_ln_in=(i == 0), alpha=alpha)
        xs, ss, pools = _mixer(xs, state_hgrn[i].astype(F32), pool0s, mixer_consts, G=gs, TT=tts, C=cs,
                               offset=PAST_LEN, layer=i, apply_ln_in=(i == 0), alpha=alpha)
        xp = _ffn(xp.reshape(bp * tp, D_MODEL), p_prompt[i].reshape(bp * tp, -1), ffn_consts,
                  TM=min(FFN_ROWS, bp * tp), alpha=alpha).reshape(bp, tp, D_MODEL)
        xs = _ffn(xs.reshape(bs * ts, D_MODEL), p_sample[i].reshape(bs * ts, -1), ffn_consts,
                  TM=min(FFN_ROWS, bs * ts), alpha=alpha).reshape(bs, ts, D_MODEL)
        hp.append(sp)
        pp.append(poolp[:, POOL_CARRY - POOL_BUF:])
        hs_.append(ss)
        ps.append(pools[:, POOL_CARRY - POOL_BUF:])
    return (xp, xs, jnp.stack(hp, axis=0), jnp.stack(pp, axis=0), jnp.stack(hs_, axis=0), jnp.stack(ps, axis=0))
```

```python
import functools

import jax
import jax.numpy as jnp
from jax import lax
from jax.experimental import pallas as pl
from jax.experimental.pallas import tpu as pltpu

F32 = jnp.float32
BF16 = jnp.bfloat16

D_MODEL = 1024
HG_HEADS = 8
HG_DK = 128
HG_DV = 128
HG_WIDTH = HG_HEADS * HG_DV
POOL_WINDOWS = (2, 4, 8, 16)
POOL_GC = 128
POOL_WIDTH = len(POOL_WINDOWS) * POOL_GC
POOL_BUF = 15
POOL_CARRY = 16
LN_EPS = 1e-5
RMS_EPS = 1e-6
EXP_CLAMP = 80.0
VMEM_LIMIT_BYTES = 56 * 1024 * 1024
CHUNK = 64
PAST_LEN = 1024
MIXER_ROWS = 256
MIXER_CHUNKS = 2
FILL_COLS = 256
FFN_ROWS = 1024
FFN_SUB_ROWS = 256


def _seqs_per_block(nseq, chunk):
    g = max(1, min(nseq, MIXER_ROWS // chunk))
    while nseq % g:
        g -= 1
    return g


_Q0, _F0, _I0, _G0 = 0, HG_WIDTH, 2 * HG_WIDTH, 3 * HG_WIDTH
_V0 = 4 * HG_WIDTH
_GA0 = _V0 + POOL_WIDTH
_GB0 = _GA0 + D_MODEL
IN_COLS = _GB0 + D_MODEL


def _sigmoid(x):
    return 0.5 * jnp.tanh(0.5 * x) + 0.5


def _layer_norm(x, g, b):
    mu = jnp.mean(x, axis=-1, keepdims=True)
    xc = x - mu
    var = jnp.mean(xc * xc, axis=-1, keepdims=True)
    return xc * lax.rsqrt(var + LN_EPS) * g + b


def _dot(a, b):
    return jnp.dot(a, b, preferred_element_type=F32)


def _mixer_kernel(x_ref, s0_ref, pool0_ref, lng_ref, lnb_ref, lbl_ref, win_ref, hgg_ref, wa_ref,
                  wpm_ref, psc_ref, wb_ref, wo_ref, l1g_ref, l1b_ref,
                  y_ref, sout_ref, poolout_ref,
                  b_s, qd_s, kdm_s, kde_s, iv_s, o_s, pm_s, hist_s, st_s,
                  *, G, TT, C, offset, layer, apply_ln_in, alpha):
    tb = pl.program_id(1)
    ntb = pl.num_programs(1)
    R = G * TT
    n_chunks = TT // C
    pairs = [(g, h) for g in range(G) for h in range(HG_HEADS)]
    hs_of = lambda h: pl.ds(h * HG_DK, HG_DK)
    rows_of = lambda g, ch: pl.ds(g * TT + ch * C, C)
    end_row = lambda g, ch: pl.ds(g * TT + ch * C + C - 1, 1)
    nt_dims = (((1,), (1,)), ((), ()))
    tn_dims = (((0,), (0,)), ((), ()))

    @pl.when(tb == 0)
    def _():
        for g, h in pairs:
            st_s[g, h] = s0_ref[g, h].T
        for g in range(G):
            hist_s[g] = pool0_ref[g]

    halves = [(r0, R // 2) for r0 in (0, R // 2)] if G % 2 == 0 else [(0, R)]
    xn_h, q_h, fr_h = [], [], []
    for r0, nr in halves:
        x_i = x_ref[pl.ds(r0 // TT, nr // TT)].reshape(nr, D_MODEL)
        xn_i = _layer_norm(x_i, lng_ref[...], lnb_ref[...]) if apply_ln_in else x_i
        xb_i = xn_i.astype(BF16)
        xn_h.append(xn_i)
        q_h.append(_dot(xb_i, win_ref[:, _Q0:_F0]))
        fr_h.append(_dot(xb_i, win_ref[:, _F0:_I0]))
    xn = jnp.concatenate(xn_h, axis=0)
    xb = xn.astype(BF16)

    def proj(lo, hi):
        return _dot(xb, win_ref[:, lo:hi])

    lbl = lbl_ref[...]
    lbe = jnp.exp(lbl - jnp.max(lbl, axis=0, keepdims=True))
    lb = jnp.sum(lbe[0:layer + 1], axis=0, keepdims=True) / jnp.sum(lbe, axis=0, keepdims=True)

    q = jnp.concatenate(q_h, axis=0)
    f = lb + (1.0 - lb) * _sigmoid(jnp.concatenate(fr_h, axis=0))
    logf = jnp.log(f)
    k = 1.0 - f
    iv_s[...] = proj(_I0, _G0).astype(BF16)

    rr = lax.broadcasted_iota(jnp.int32, (TT, TT), 0)
    cc = lax.broadcasted_iota(jnp.int32, (TT, TT), 1)
    shift = C.bit_length() - 1
    tri_blk = jnp.where(((rr >> shift) == (cc >> shift)) & (cc <= rr), 1.0, 0.0).astype(BF16)
    logf_hi = logf.astype(BF16)
    logf_lo = (logf - logf_hi.astype(F32)).astype(BF16)
    b = jnp.concatenate(
        [_dot(tri_blk, logf_hi[g * TT:(g + 1) * TT]) + _dot(tri_blk, logf_lo[g * TT:(g + 1) * TT]) for g in range(G)],
        axis=0)
    b_s[...] = b

    bc = jnp.concatenate(
        [jnp.broadcast_to(b_s[end_row(g, ch), :], (C, HG_WIDTH)) for g in range(G) for ch in range(n_chunks)], axis=0)
    bm = 0.5 * bc
    qd_s[...] = (q * jnp.exp(jnp.minimum(b - bm, EXP_CLAMP))).astype(BF16)
    kdm_s[...] = (k * jnp.exp(jnp.minimum(bm - b, EXP_CLAMP))).astype(BF16)
    kde_s[...] = (k * jnp.exp(bc - b)).astype(BF16)

    dense = {}

    def proj_piece(name, lo, hi):
        def run():
            dense.setdefault(name, []).append(proj(lo, hi))
        return run

    def pool_piece():
        v = jnp.concatenate(dense["v"], axis=1)
        pos1 = (offset + 1 + tb * TT + lax.broadcasted_iota(jnp.int32, (TT, 1), 0)).astype(F32)
        wins = [[] for _ in POOL_WINDOWS]
        for g in range(G):
            ext = jnp.concatenate([hist_s[g], v[g * TT:(g + 1) * TT]], axis=0)
            tail = ext[TT:TT + POOL_CARRY]
            poolout_ref[g] = tail
            hist_s[g] = tail
            acc = ext
            for j, w_len in enumerate(POOL_WINDOWS):
                acc = acc + pltpu.roll(acc, w_len // 2, 0)
                wins[j].append(acc[POOL_CARRY:, :POOL_GC])
                acc = acc[:, POOL_GC:]
        for j, w_len in enumerate(POOL_WINDOWS):
            js = pl.ds(j * POOL_GC, POOL_GC)
            win = jnp.concatenate(wins[j], axis=0) if G > 1 else wins[j][0]
            cnt = jnp.minimum(pos1, float(w_len))
            cnt = jnp.concatenate([cnt] * G, axis=0) if G > 1 else cnt
            pooled = win / cnt - v[:, j * POOL_GC:(j + 1) * POOL_GC]
            pm_s[:, js] = (_dot(pooled.astype(BF16), wpm_ref[j]) * psc_ref[:, js]).astype(BF16)

    def wb_piece(lo, hi):
        def run():
            dense.setdefault("yb", []).append(_dot(pm_s[...], wb_ref[:, lo:hi]))
        return run

    pieces = [proj_piece("gate", _G0 + j, _G0 + j + FILL_COLS) for j in range(0, HG_WIDTH, FILL_COLS)]
    pieces += [proj_piece("ga", _GA0 + j, _GA0 + j + FILL_COLS) for j in range(0, D_MODEL, FILL_COLS)]
    pieces += [proj_piece("v", _V0 + j, _V0 + j + FILL_COLS) for j in range(0, POOL_WIDTH, FILL_COLS)]
    pieces += [pool_piece]
    pieces += [proj_piece("gb", _GB0 + j, _GB0 + j + FILL_COLS) for j in range(0, D_MODEL, FILL_COLS)]
    pieces += [wb_piece(j, j + FILL_COLS) for j in range(0, D_MODEL, FILL_COLS)]
    slots = 3 * len(pairs) * n_chunks
    cadence = max(1, slots // len(pieces))
    issued = [0]

    def after_pair():
        issued[0] += 1
        if issued[0] % cadence == 0 and pieces:
            pieces.pop(0)()

    ti = lax.broadcasted_iota(jnp.int32, (C, C), 0)
    si = lax.broadcasted_iota(jnp.int32, (C, C), 1)
    causal = si <= ti
    for ch in range(n_chunks):
        res = {}
        for g, h in pairs:
            rows, hs = rows_of(g, ch), hs_of(h)
            em = jnp.exp(0.5 * b_s[end_row(g, ch), hs])
            w = jnp.concatenate([(st_s[g, h] * em).astype(BF16), kdm_s[rows, hs]], axis=0)
            res[g, h] = lax.dot_general(qd_s[rows, hs], w, nt_dims, preferred_element_type=F32)
            after_pair()
        for g, h in pairs:
            rows, hs = rows_of(g, ch), hs_of(h)
            ec = jnp.exp(b_s[end_row(g, ch), hs])
            st_s[g, h] = st_s[g, h] * ec + lax.dot_general(iv_s[rows, hs], kde_s[rows, hs], tn_dims,
                                                           preferred_element_type=F32)
            after_pair()
        for g, h in pairs:
            rows, hs = rows_of(g, ch), hs_of(h)
            p = jnp.where(causal, res[g, h][:, HG_DV:], 0.0).astype(BF16)
            o_s[rows, hs] = _dot(p, iv_s[rows, hs]) + res[g, h][:, :HG_DV]
            after_pair()
    while pieces:
        pieces.pop(0)()

    ones_blk = jnp.ones((HG_DV, HG_DV), BF16)
    for h in range(HG_HEADS):
        hs = hs_of(h)
        oh = o_s[:, hs]
        ss = _dot((oh * oh).astype(BF16), ones_blk)
        o_s[:, hs] = oh * lax.rsqrt(ss * (1.0 / HG_DV) + RMS_EPS) * hgg_ref[:, hs]

    gate = jnp.concatenate(dense["gate"], axis=1)
    gate = gate * _sigmoid(gate)
    yb = _sigmoid(jnp.concatenate(dense["gb"], axis=1)) * jnp.concatenate(dense["yb"], axis=1)
    ga = _sigmoid(jnp.concatenate(dense["ga"], axis=1))
    for r0, nr in halves:
        rs = slice(r0, r0 + nr)
        ya = _dot((o_s[pl.ds(r0, nr), :] * gate[rs]).astype(BF16), wa_ref[...])
        m = ga[rs] * ya + yb[rs]
        z = alpha * xn[rs] + _dot(m.astype(BF16), wo_ref[...])
        y_ref[pl.ds(r0 // TT, nr // TT)] = _layer_norm(z, l1g_ref[...], l1b_ref[...]).reshape(nr // TT, TT, D_MODEL)

    @pl.when(tb == ntb - 1)
    def _():
        for g, h in pairs:
            sout_ref[g, h] = st_s[g, h].T


def _ffn_kernel(x_ref, p_ref, wup_ref, wdn_ref, wpg_ref, wpp_ref, g_ref, b_ref, y_ref, *, d_ff, alpha):
    for r0 in range(0, x_ref.shape[0], FFN_SUB_ROWS):
        rows = pl.ds(r0, FFN_SUB_ROWS)
        x = x_ref[rows, :]
        xb = x.astype(BF16)
        gt = _dot(xb, wup_ref[:, :d_ff])
        up = _dot(xb, wup_ref[:, d_ff:])
        hid = (gt * _sigmoid(gt) * up).astype(BF16)
        ffn = _dot(hid, wdn_ref[...])
        ple = _sigmoid(_dot(xb, wpg_ref[...])) * _dot(p_ref[rows, :].astype(BF16), wpp_ref[...])
        y_ref[rows, :] = _layer_norm(alpha * x + ffn + ple, g_ref[...], b_ref[...])


def _const_spec(shape):
    zeros = (0,) * len(shape)
    return pl.BlockSpec(shape, lambda *_: zeros, pipeline_mode=pl.Buffered(1))


def _mixer(x, s0, pool0, consts, *, G, TT, C, offset, layer, apply_ln_in, alpha):
    nseq, T, _ = x.shape
    assert nseq % G == 0 and T % TT == 0 and TT % C == 0 and C % 16 == 0
    grid = (nseq // G, T // TT)
    kern = functools.partial(_mixer_kernel, G=G, TT=TT, C=C, offset=offset, layer=layer,
                             apply_ln_in=apply_ln_in, alpha=alpha)
    in_specs = [
        pl.BlockSpec((G, TT, D_MODEL), lambda i, t: (i, t, 0)),
        pl.BlockSpec((G, HG_HEADS, HG_DK, HG_DV), lambda i, t: (i, 0, 0, 0)),
        pl.BlockSpec((G, POOL_CARRY, POOL_WIDTH), lambda i, t: (i, 0, 0)),
    ] + [_const_spec(c.shape) for c in consts]
    out_specs = [
        pl.BlockSpec((G, TT, D_MODEL), lambda i, t: (i, t, 0)),
        pl.BlockSpec((G, HG_HEADS, HG_DK, HG_DV), lambda i, t: (i, 0, 0, 0)),
        pl.BlockSpec((G, POOL_CARRY, POOL_WIDTH), lambda i, t: (i, 0, 0)),
    ]
    out_shape = [
        jax.ShapeDtypeStruct((nseq, T, D_MODEL), F32),
        jax.ShapeDtypeStruct((nseq, HG_HEADS, HG_DK, HG_DV), F32),
        jax.ShapeDtypeStruct((nseq, POOL_CARRY, POOL_WIDTH), F32),
    ]
    rows = G * TT
    scratch = [
        pltpu.VMEM((rows, HG_WIDTH), F32),
        pltpu.VMEM((rows, HG_WIDTH), BF16),
        pltpu.VMEM((rows, HG_WIDTH), BF16),
        pltpu.VMEM((rows, HG_WIDTH), BF16),
        pltpu.VMEM((rows, HG_WIDTH), BF16),
        pltpu.VMEM((rows, HG_WIDTH), F32),
        pltpu.VMEM((rows, POOL_WIDTH), BF16),
        pltpu.VMEM((G, POOL_CARRY, POOL_WIDTH), F32),
        pltpu.VMEM((G, HG_HEADS, HG_DV, HG_DK), F32),
    ]
    return pl.pallas_call(
        kern,
        grid=grid,
        in_specs=in_specs,
        out_specs=out_specs,
        out_shape=out_shape,
        scratch_shapes=scratch,
        compiler_params=pltpu.CompilerParams(
            dimension_semantics=("parallel", "arbitrary"), vmem_limit_bytes=VMEM_LIMIT_BYTES),
        name="mixer",
    )(x, s0, pool0, *consts)


def _ffn(x, p, consts, *, TM, alpha):
    n, _ = x.shape
    assert n % TM == 0 and TM % FFN_SUB_ROWS == 0
    d_ff = consts[1].shape[0]
    kern = functools.partial(_ffn_kernel, d_ff=d_ff, alpha=alpha)
    return pl.pallas_call(
        kern,
        grid=(n // TM,),
        in_specs=[pl.BlockSpec((TM, D_MODEL), lambda i: (i, 0)),
                  pl.BlockSpec((TM, p.shape[1]), lambda i: (i, 0))] + [_const_spec(c.shape) for c in consts],
        out_specs=pl.BlockSpec((TM, D_MODEL), lambda i: (i, 0)),
        out_shape=jax.ShapeDtypeStruct((n, D_MODEL), F32),
        compiler_params=pltpu.CompilerParams(
            dimension_semantics=("parallel",), vmem_limit_bytes=VMEM_LIMIT_BYTES),
        name="ffn",
    )(x, p, *consts)


def kernel(x_prompt, x_sample, p_prompt, p_sample, state_hgrn, state_pool, ln_in_g, ln_in_b, lb_logits, w_in,
           hgrn_norm_g, w_branch_a, w_pool_mix, pool_scale, w_branch_b, w_out, ln1_g, ln1_b, w_ffn_up,
           w_ffn_down, w_ple_proj, w_ple_gate, ln2_g, ln2_b):
    depth = w_in.shape[0]
    alpha = float((2 * depth) ** 0.25)
    bp, tp, _ = x_prompt.shape
    bs, ts, _ = x_sample.shape
    assert w_in.shape[1:] == (D_MODEL, IN_COLS) and x_prompt.shape[2] == x_sample.shape[2] == D_MODEL
    assert state_hgrn.shape == (depth, bs, HG_HEADS, HG_DK, HG_DV)
    assert state_pool.shape == (depth, bs, POOL_BUF, POOL_WIDTH) and lb_logits.shape == (depth + 1, HG_WIDTH)
    cp, cs = min(tp, CHUNK), min(ts, CHUNK)
    gp, gs = _seqs_per_block(bp, cp), _seqs_per_block(bs, cs)
    ttp, tts = cp * min(MIXER_CHUNKS, tp // cp), cs * min(MIXER_CHUNKS, ts // cs)
    row = lambda a: a.reshape(1, -1).astype(F32)

    xp, xs = x_prompt, x_sample
    hp, pp, hs_, ps = [], [], [], []
    for i in range(depth):
        mixer_consts = (row(ln_in_g), row(ln_in_b), lb_logits.astype(F32), w_in[i].astype(BF16),
                        row(hgrn_norm_g[i]), w_branch_a[i].astype(BF16), w_pool_mix[i].astype(BF16),
                        row(pool_scale[i]), w_branch_b[i].astype(BF16), w_out[i].astype(BF16),
                        row(ln1_g[i]), row(ln1_b[i]))
        ffn_consts = (w_ffn_up[i].astype(BF16), w_ffn_down[i].astype(BF16), w_ple_gate[i].astype(BF16),
                      w_ple_proj[i].astype(BF16), row(ln2_g[i]), row(ln2_b[i]))
        s0p = jnp.zeros((bp, HG_HEADS, HG_DK, HG_DV), F32)
        pool0p = jnp.zeros((bp, POOL_CARRY, POOL_WIDTH), F32)
        pool0s = jnp.pad(state_pool[i].astype(F32), ((0, 0), (POOL_CARRY - POOL_BUF, 0), (0, 0)))

        xp, sp, poolp = _mixer(xp, s0p, pool0p, mixer_consts, G=gp, TT=ttp, C=cp, offset=0, layer=i,
                               apply_ln_in=(i == 0), alpha=alpha)
        xs, ss, pools = _mixer(xs, state_hgrn[i].astype(F32), pool0s, mixer_consts, G=gs, TT=tts, C=cs,
                               offset=PAST_LEN, layer=i, apply_ln_in=(i == 0), alpha=alpha)
        xp = _ffn(xp.reshape(bp * tp, D_MODEL), p_prompt[i].reshape(bp * tp, -1), ffn_consts,
                  TM=min(FFN_ROWS, bp * tp), alpha=alpha).reshape(bp, tp, D_MODEL)
        xs = _ffn(xs.reshape(bs * ts, D_MODEL), p_sample[i].reshape(bs * ts, -1), ffn_consts,
                  TM=min(FFN_ROWS, bs * ts), alpha=alpha).reshape(bs, ts, D_MODEL)
        hp.append(sp)
        pp.append(poolp[:, POOL_CARRY - POOL_BUF:])
        hs_.append(ss)
        ps.append(pools[:, POOL_CARRY - POOL_BUF:])
    return (xp, xs, jnp.stack(hp, axis=0), jnp.stack(pp, axis=0), jnp.stack(hs_, axis=0), jnp.stack(ps, axis=0))
```

```python
import functools

import jax
import jax.numpy as jnp
from jax import lax
from jax.experimental import pallas as pl
from jax.experimental.pallas import tpu as pltpu

F32 = jnp.float32
BF16 = jnp.bfloat16

D_MODEL = 1024
HG_HEADS = 8
HG_DK = 128
HG_DV = 128
HG_WIDTH = HG_HEADS * HG_DV
POOL_WINDOWS = (2, 4, 8, 16)
POOL_GC = 128
POOL_WIDTH = len(POOL_WINDOWS) * POOL_GC
POOL_BUF = 15
POOL_CARRY = 16
LN_EPS = 1e-5
RMS_EPS = 1e-6
EXP_CLAMP = 80.0
VMEM_LIMIT_BYTES = 56 * 1024 * 1024
CHUNK = 64
PAST_LEN = 1024
MIXER_ROWS = 256
MIXER_CHUNKS = 2
FILL_COLS = 256
FFN_ROWS = 1024
FFN_SUB_ROWS = 256


def _seqs_per_block(nseq, chunk):
    g = max(1, min(nseq, MIXER_ROWS // chunk))
    while nseq % g:
        g -= 1
    return g


_Q0, _F0, _I0, _G0 = 0, HG_WIDTH, 2 * HG_WIDTH, 3 * HG_WIDTH
_V0 = 4 * HG_WIDTH
_GA0 = _V0 + POOL_WIDTH
_GB0 = _GA0 + D_MODEL
IN_COLS = _GB0 + D_MODEL


def _sigmoid(x):
    return 0.5 * jnp.tanh(0.5 * x) + 0.5


def _layer_norm(x, g, b):
    mu = jnp.mean(x, axis=-1, keepdims=True)
    xc = x - mu
    var = jnp.mean(xc * xc, axis=-1, keepdims=True)
    return xc * lax.rsqrt(var + LN_EPS) * g + b


def _dot(a, b):
    return jnp.dot(a, b, preferred_element_type=F32)


def _mixer_kernel(x_ref, s0_ref, pool0_ref, lng_ref, lnb_ref, lbl_ref, win_ref, hgg_ref, wa_ref,
                  wpm_ref, psc_ref, wb_ref, wo_ref, l1g_ref, l1b_ref,
                  y_ref, sout_ref, poolout_ref,
                  b_s, qd_s, kdm_s, kde_s, iv_s, o_s, pm_s, hist_s, st_s,
                  *, G, TT, C, offset, layer, apply_ln_in, alpha):
    tb = pl.program_id(1)
    ntb = pl.num_programs(1)
    R = G * TT
    n_chunks = TT // C
    pairs = [(g, h) for g in range(G) for h in range(HG_HEADS)]
    hs_of = lambda h: pl.ds(h * HG_DK, HG_DK)
    rows_of = lambda g, ch: pl.ds(g * TT + ch * C, C)
    end_row = lambda g, ch: pl.ds(g * TT + ch * C + C - 1, 1)
    nt_dims = (((1,), (1,)), ((), ()))
    tn_dims = (((0,), (0,)), ((), ()))

    @pl.when(tb == 0)
    def _():
        for g, h in pairs:
            st_s[g, h] = s0_ref[g, h].T
        for g in range(G):
            hist_s[g] = pool0_ref[g]

    x = x_ref[...].reshape(R, D_MODEL)
    xn = _layer_norm(x, lng_ref[...], lnb_ref[...]) if apply_ln_in else x
    xb = xn.astype(BF16)

    def proj(lo, hi):
        return _dot(xb, win_ref[:, lo:hi])

    lbl = lbl_ref[...]
    lbe = jnp.exp(lbl - jnp.max(lbl, axis=0, keepdims=True))
    lb = jnp.sum(lbe[0:layer + 1], axis=0, keepdims=True) / jnp.sum(lbe, axis=0, keepdims=True)

    q = proj(_Q0, _F0)
    f = lb + (1.0 - lb) * _sigmoid(proj(_F0, _I0))
    logf = jnp.log(f)
    k = 1.0 - f
    iv_s[...] = proj(_I0, _G0).astype(BF16)

    rr = lax.broadcasted_iota(jnp.int32, (TT, TT), 0)
    cc = lax.broadcasted_iota(jnp.int32, (TT, TT), 1)
    shift = C.bit_length() - 1
    tri_blk = jnp.where(((rr >> shift) == (cc >> shift)) & (cc <= rr), 1.0, 0.0).astype(BF16)
    logf_hi = logf.astype(BF16)
    logf_lo = (logf - logf_hi.astype(F32)).astype(BF16)
    b = jnp.concatenate(
        [_dot(tri_blk, logf_hi[g * TT:(g + 1) * TT]) + _dot(tri_blk, logf_lo[g * TT:(g + 1) * TT]) for g in range(G)],
        axis=0)
    b_s[...] = b

    bc = jnp.concatenate(
        [jnp.broadcast_to(b_s[end_row(g, ch), :], (C, HG_WIDTH)) for g in range(G) for ch in range(n_chunks)], axis=0)
    bm = 0.5 * bc
    qd_s[...] = (q * jnp.exp(jnp.minimum(b - bm, EXP_CLAMP))).astype(BF16)
    kdm_s[...] = (k * jnp.exp(jnp.minimum(bm - b, EXP_CLAMP))).astype(BF16)
    kde_s[...] = (k * jnp.exp(bc - b)).astype(BF16)

    dense = {}

    def proj_piece(name, lo, hi):
        def run():
            dense.setdefault(name, []).append(proj(lo, hi))
        return run

    def pool_piece():
        v = jnp.concatenate(dense["v"], axis=1)
        pos1 = (offset + 1 + tb * TT + lax.broadcasted_iota(jnp.int32, (TT, 1), 0)).astype(F32)
        wins = [[] for _ in POOL_WINDOWS]
        for g in range(G):
            ext = jnp.concatenate([hist_s[g], v[g * TT:(g + 1) * TT]], axis=0)
            tail = ext[TT:TT + POOL_CARRY]
            poolout_ref[g] = tail
            hist_s[g] = tail
            acc = ext
            for j, w_len in enumerate(POOL_WINDOWS):
                acc = acc + pltpu.roll(acc, w_len // 2, 0)
                wins[j].append(acc[POOL_CARRY:, :POOL_GC])
                acc = acc[:, POOL_GC:]
        for j, w_len in enumerate(POOL_WINDOWS):
            js = pl.ds(j * POOL_GC, POOL_GC)
            win = jnp.concatenate(wins[j], axis=0) if G > 1 else wins[j][0]
            cnt = jnp.minimum(pos1, float(w_len))
            cnt = jnp.concatenate([cnt] * G, axis=0) if G > 1 else cnt
            pooled = win / cnt - v[:, j * POOL_GC:(j + 1) * POOL_GC]
            pm_s[:, js] = (_dot(pooled.astype(BF16), wpm_ref[j]) * psc_ref[:, js]).astype(BF16)

    def wb_piece(lo, hi):
        def run():
            dense.setdefault("yb", []).append(_dot(pm_s[...], wb_ref[:, lo:hi]))
        return run

    pieces = [proj_piece("gate", _G0 + j, _G0 + j + FILL_COLS) for j in range(0, HG_WIDTH, FILL_COLS)]
    pieces += [proj_piece("ga", _GA0 + j, _GA0 + j + FILL_COLS) for j in range(0, D_MODEL, FILL_COLS)]
    pieces += [proj_piece("v", _V0 + j, _V0 + j + FILL_COLS) for j in range(0, POOL_WIDTH, FILL_COLS)]
    pieces += [pool_piece]
    pieces += [proj_piece("gb", _GB0 + j, _GB0 + j + FILL_COLS) for j in range(0, D_MODEL, FILL_COLS)]
    pieces += [wb_piece(j, j + FILL_COLS) for j in range(0, D_MODEL, FILL_COLS)]
    slots = 3 * len(pairs) * n_chunks
    cadence = max(1, slots // len(pieces))
    issued = [0]

    def after_pair():
        issued[0] += 1
        if issued[0] % cadence == 0 and pieces:
            pieces.pop(0)()

    ti = lax.broadcasted_iota(jnp.int32, (C, C), 0)
    si = lax.broadcasted_iota(jnp.int32, (C, C), 1)
    causal = si <= ti
    for ch in range(n_chunks):
        res = {}
        for g, h in pairs:
            rows, hs = rows_of(g, ch), hs_of(h)
            em = jnp.exp(0.5 * b_s[end_row(g, ch), hs])
            w = jnp.concatenate([(st_s[g, h] * em).astype(BF16), kdm_s[rows, hs]], axis=0)
            res[g, h] = lax.dot_general(qd_s[rows, hs], w, nt_dims, preferred_element_type=F32)
            after_pair()
        for g, h in pairs:
            rows, hs = rows_of(g, ch), hs_of(h)
            ec = jnp.exp(b_s[end_row(g, ch), hs])
            st_s[g, h] = st_s[g, h] * ec + lax.dot_general(iv_s[rows, hs], kde_s[rows, hs], tn_dims,
                                                           preferred_element_type=F32)
            after_pair()
        for g, h in pairs:
            rows, hs = rows_of(g, ch), hs_of(h)
            p = jnp.where(causal, res[g, h][:, HG_DV:], 0.0).astype(BF16)
            o_s[rows, hs] = _dot(p, iv_s[rows, hs]) + res[g, h][:, :HG_DV]
            after_pair()
    while pieces:
        pieces.pop(0)()

    ones_blk = jnp.ones((HG_DV, HG_DV), BF16)
    for h in range(HG_HEADS):
        hs = hs_of(h)
        oh = o_s[:, hs]
        ss = _dot((oh * oh).astype(BF16), ones_blk)
        o_s[:, hs] = oh * lax.rsqrt(ss * (1.0 / HG_DV) + RMS_EPS) * hgg_ref[:, hs]

    gate = jnp.concatenate(dense["gate"], axis=1)
    ya = _dot((o_s[...] * (gate * _sigmoid(gate))).astype(BF16), wa_ref[...])
    yb = _sigmoid(jnp.concatenate(dense["gb"], axis=1)) * jnp.concatenate(dense["yb"], axis=1)
    m = _sigmoid(jnp.concatenate(dense["ga"], axis=1)) * ya + yb
    z = alpha * xn + _dot(m.astype(BF16), wo_ref[...])
    y_ref[...] = _layer_norm(z, l1g_ref[...], l1b_ref[...]).reshape(G, TT, D_MODEL)

    @pl.when(tb == ntb - 1)
    def _():
        for g, h in pairs:
            sout_ref[g, h] = st_s[g, h].T


def _ffn_kernel(xa_ref, pa_ref, xb_ref, pb_ref, wup_ref, wdn_ref, wpg_ref, wpp_ref, g_ref, b_ref, ya_ref, yb_ref,
                *, n_a, d_ff, alpha):
    step = pl.program_id(0)

    @pl.when(step < n_a)
    def _():
        _ffn_block(xa_ref, pa_ref, wup_ref, wdn_ref, wpg_ref, wpp_ref, g_ref, b_ref, ya_ref, d_ff=d_ff, alpha=alpha)

    @pl.when(step >= n_a)
    def _():
        _ffn_block(xb_ref, pb_ref, wup_ref, wdn_ref, wpg_ref, wpp_ref, g_ref, b_ref, yb_ref, d_ff=d_ff, alpha=alpha)


def _ffn_block(x_ref, p_ref, wup_ref, wdn_ref, wpg_ref, wpp_ref, g_ref, b_ref, y_ref, *, d_ff, alpha):
    for r0 in range(0, x_ref.shape[0], FFN_SUB_ROWS):
        rows = pl.ds(r0, FFN_SUB_ROWS)
        x = x_ref[rows, :]
        xb = x.astype(BF16)
        gt = _dot(xb, wup_ref[:, :d_ff])
        up = _dot(xb, wup_ref[:, d_ff:])
        hid = (gt * _sigmoid(gt) * up).astype(BF16)
        ffn = _dot(hid, wdn_ref[...])
        ple = _sigmoid(_dot(xb, wpg_ref[...])) * _dot(p_ref[rows, :].astype(BF16), wpp_ref[...])
        y_ref[rows, :] = _layer_norm(alpha * x + ffn + ple, g_ref[...], b_ref[...])


def _const_spec(shape):
    zeros = (0,) * len(shape)
    return pl.BlockSpec(shape, lambda *_: zeros, pipeline_mode=pl.Buffered(1))


def _mixer(x, s0, pool0, consts, *, G, TT, C, offset, layer, apply_ln_in, alpha):
    nseq, T, _ = x.shape
    assert nseq % G == 0 and T % TT == 0 and TT % C == 0 and C % 16 == 0
    grid = (nseq // G, T // TT)
    kern = functools.partial(_mixer_kernel, G=G, TT=TT, C=C, offset=offset, layer=layer,
                             apply_ln_in=apply_ln_in, alpha=alpha)
    in_specs = [
        pl.BlockSpec((G, TT, D_MODEL), lambda i, t: (i, t, 0)),
        pl.BlockSpec((G, HG_HEADS, HG_DK, HG_DV), lambda i, t: (i, 0, 0, 0)),
        pl.BlockSpec((G, POOL_CARRY, POOL_WIDTH), lambda i, t: (i, 0, 0)),
    ] + [_const_spec(c.shape) for c in consts]
    out_specs = [
        pl.BlockSpec((G, TT, D_MODEL), lambda i, t: (i, t, 0)),
        pl.BlockSpec((G, HG_HEADS, HG_DK, HG_DV), lambda i, t: (i, 0, 0, 0)),
        pl.BlockSpec((G, POOL_CARRY, POOL_WIDTH), lambda i, t: (i, 0, 0)),
    ]
    out_shape = [
        jax.ShapeDtypeStruct((nseq, T, D_MODEL), F32),
        jax.ShapeDtypeStruct((nseq, HG_HEADS, HG_DK, HG_DV), F32),
        jax.ShapeDtypeStruct((nseq, POOL_CARRY, POOL_WIDTH), F32),
    ]
    rows = G * TT
    scratch = [
        pltpu.VMEM((rows, HG_WIDTH), F32),
        pltpu.VMEM((rows, HG_WIDTH), BF16),
        pltpu.VMEM((rows, HG_WIDTH), BF16),
        pltpu.VMEM((rows, HG_WIDTH), BF16),
        pltpu.VMEM((rows, HG_WIDTH), BF16),
        pltpu.VMEM((rows, HG_WIDTH), F32),
        pltpu.VMEM((rows, POOL_WIDTH), BF16),
        pltpu.VMEM((G, POOL_CARRY, POOL_WIDTH), F32),
        pltpu.VMEM((G, HG_HEADS, HG_DV, HG_DK), F32),
    ]
    return pl.pallas_call(
        kern,
        grid=grid,
        in_specs=in_specs,
        out_specs=out_specs,
        out_shape=out_shape,
        scratch_shapes=scratch,
        compiler_params=pltpu.CompilerParams(
            dimension_semantics=("parallel", "arbitrary"), vmem_limit_bytes=VMEM_LIMIT_BYTES),
        name="mixer",
    )(x, s0, pool0, *consts)


def _ffn(xa, pa, xb, pb, consts, *, alpha):
    (na, _), (nb, _) = xa.shape, xb.shape
    tma, tmb = min(FFN_ROWS, na), min(FFN_ROWS, nb)
    assert na % tma == 0 and nb % tmb == 0 and tma % FFN_SUB_ROWS == 0 and tmb % FFN_SUB_ROWS == 0
    n_a, n_b = na // tma, nb // tmb
    d_ff = consts[1].shape[0]
    kern = functools.partial(_ffn_kernel, n_a=n_a, d_ff=d_ff, alpha=alpha)
    a_map = lambda i: (jnp.minimum(i, n_a - 1), 0)
    b_map = lambda i: (jnp.maximum(i - n_a, 0), 0)
    return pl.pallas_call(
        kern,
        grid=(n_a + n_b,),
        in_specs=[pl.BlockSpec((tma, D_MODEL), a_map), pl.BlockSpec((tma, pa.shape[1]), a_map),
                  pl.BlockSpec((tmb, D_MODEL), b_map, pipeline_mode=pl.Buffered(1)),
                  pl.BlockSpec((tmb, pb.shape[1]), b_map, pipeline_mode=pl.Buffered(1))]
        + [_const_spec(c.shape) for c in consts],
        out_specs=[pl.BlockSpec((tma, D_MODEL), a_map), pl.BlockSpec((tmb, D_MODEL), b_map)],
        out_shape=[jax.ShapeDtypeStruct((na, D_MODEL), F32), jax.ShapeDtypeStruct((nb, D_MODEL), F32)],
        compiler_params=pltpu.CompilerParams(
            dimension_semantics=("arbitrary",), vmem_limit_bytes=VMEM_LIMIT_BYTES),
        name="ffn",
    )(xa, pa, xb, pb, *consts)


def kernel(x_prompt, x_sample, p_prompt, p_sample, state_hgrn, state_pool, ln_in_g, ln_in_b, lb_logits, w_in,
           hgrn_norm_g, w_branch_a, w_pool_mix, pool_scale, w_branch_b, w_out, ln1_g, ln1_b, w_ffn_up,
           w_ffn_down, w_ple_proj, w_ple_gate, ln2_g, ln2_b):
    depth = w_in.shape[0]
    alpha = float((2 * depth) ** 0.25)
    bp, tp, _ = x_prompt.shape
    bs, ts, _ = x_sample.shape
    assert w_in.shape[1:] == (D_MODEL, IN_COLS) and x_prompt.shape[2] == x_sample.shape[2] == D_MODEL
    assert state_hgrn.shape == (depth, bs, HG_HEADS, HG_DK, HG_DV)
    assert state_pool.shape == (depth, bs, POOL_BUF, POOL_WIDTH) and lb_logits.shape == (depth + 1, HG_WIDTH)
    cp, cs = min(tp, CHUNK), min(ts, CHUNK)
    gp, gs = _seqs_per_block(bp, cp), _seqs_per_block(bs, cs)
    ttp, tts = cp * min(MIXER_CHUNKS, tp // cp), cs * min(MIXER_CHUNKS, ts // cs)
    row = lambda a: a.reshape(1, -1).astype(F32)

    xp, xs = x_prompt, x_sample
    hp, pp, hs_, ps = [], [], [], []
    for i in range(depth):
        mixer_consts = (row(ln_in_g), row(ln_in_b), lb_logits.astype(F32), w_in[i].astype(BF16),
                        row(hgrn_norm_g[i]), w_branch_a[i].astype(BF16), w_pool_mix[i].astype(BF16),
                        row(pool_scale[i]), w_branch_b[i].astype(BF16), w_out[i].astype(BF16),
                        row(ln1_g[i]), row(ln1_b[i]))
        ffn_consts = (w_ffn_up[i].astype(BF16), w_ffn_down[i].astype(BF16), w_ple_gate[i].astype(BF16),
                      w_ple_proj[i].astype(BF16), row(ln2_g[i]), row(ln2_b[i]))
        s0p = jnp.zeros((bp, HG_HEADS, HG_DK, HG_DV), F32)
        pool0p = jnp.zeros((bp, POOL_CARRY, POOL_WIDTH), F32)
        pool0s = jnp.pad(state_pool[i].astype(F32), ((0, 0), (POOL_CARRY - POOL_BUF, 0), (0, 0)))

        xp, sp, poolp = _mixer(xp, s0p, pool0p, mixer_consts, G=gp, TT=ttp, C=cp, offset=0, layer=i,
                               apply_ln_in=(i == 0), alpha=alpha)
        xs, ss, pools = _mixer(xs, state_hgrn[i].astype(F32), pool0s, mixer_consts, G=gs, TT=tts, C=cs,
                               offset=PAST_LEN, layer=i, apply_ln_in=(i == 0), alpha=alpha)
        xp, xs = _ffn(xp.reshape(bp * tp, D_MODEL), p_prompt[i].reshape(bp * tp, -1),
                      xs.reshape(bs * ts, D_MODEL), p_sample[i].reshape(bs * ts, -1), ffn_consts, alpha=alpha)
        xp, xs = xp.reshape(bp, tp, D_MODEL), xs.reshape(bs, ts, D_MODEL)
        hp.append(sp)
        pp.append(poolp[:, POOL_CARRY - POOL_BUF:])
        hs_.append(ss)
        ps.append(pools[:, POOL_CARRY - POOL_BUF:])
    return (xp, xs, jnp.stack(hp, axis=0), jnp.stack(pp, axis=0), jnp.stack(hs_, axis=0), jnp.stack(ps, axis=0))
```
